```python
import math, functools
import jax, jax.numpy as jnp
from jax import lax
import numpy as np

D_MODEL = 1024
BATCH = 1
SEQ = 16384
DEPTH = 2

GRID_W = 64
CTX_LEN = 256
Q_BLOCK = 128
RET_CHUNK = 128
ROPE_THETA = 10000.0
EPS = 1e-6

GQA_HEADS = 8
GQA_KV_HEADS = 2
GQA_HEAD_DIM = 64
RET_HEADS = 4
RET_QK_DIM = 64
RET_V_DIM = 128
MLA_HEADS = 8
MLA_Q_RANK = 384
MLA_KV_RANK = 256
MLA_NOPE_DIM = 64
MLA_ROPE_DIM = 32
MLA_V_DIM = 64
DIFF_HEADS = 4
DIFF_HEAD_DIM = 64
N_BRANCHES = 4
BRANCH_WIDTH = 512
FFN_HIDDEN = -(-8 * D_MODEL // (3 * 256)) * 256

IN_SPLITS = (
    GQA_HEADS * GQA_HEAD_DIM, GQA_KV_HEADS * GQA_HEAD_DIM, GQA_KV_HEADS * GQA_HEAD_DIM,
    RET_HEADS * RET_QK_DIM, RET_HEADS * RET_QK_DIM, RET_HEADS * RET_V_DIM, RET_HEADS * RET_V_DIM,
    MLA_Q_RANK, MLA_KV_RANK, MLA_ROPE_DIM,
    DIFF_HEADS * 2 * DIFF_HEAD_DIM, DIFF_HEADS * 2 * DIFF_HEAD_DIM, DIFF_HEADS * 2 * DIFF_HEAD_DIM,
    N_BRANCHES * D_MODEL,
)
IN_WIDTH = sum(IN_SPLITS)

kernel_name = 'hybrid_gqa_retention_mla_diffattn_prefix_ctx'


def rms_norm(x, gain=None):
    x32 = x.astype(jnp.float32)
    y = x32 * lax.rsqrt(jnp.mean(x32 * x32, axis=-1, keepdims=True) + EPS)
    if gain is not None:
        y = y * gain.astype(jnp.float32)
    return y.astype(x.dtype)


def axial_rope(n_rows, dim):
    quarter = dim // 4
    freqs = ROPE_THETA ** (-jnp.arange(quarter, dtype=jnp.float32) / quarter)
    row = jnp.repeat(jnp.arange(n_rows, dtype=jnp.float32), GRID_W)
    col = jnp.tile(jnp.arange(GRID_W, dtype=jnp.float32), n_rows)
    ang = jnp.concatenate([row[:, None] * freqs, col[:, None] * freqs], axis=-1)
    return jnp.cos(ang), jnp.sin(ang)


def apply_rope(x, rope):
    cos, sin = rope
    S, half = cos.shape
    shape = (S,) + (1,) * (x.ndim - 3) + (half,)
    cos = cos.reshape(shape).astype(x.dtype)
    sin = sin.reshape(shape).astype(x.dtype)
    xp = x.reshape(x.shape[:-1] + (half, 2))
    x0, x1 = xp[..., 0], xp[..., 1]
    return jnp.stack([x0 * cos - x1 * sin, x0 * sin + x1 * cos], axis=-1).reshape(x.shape)


def split_cols(a):
    out, off = [], 0
    for w in IN_SPLITS:
        out.append(a[..., off:off + w])
        off += w
    return out


def sweep_queries(block_fn, q):
    B, S = q.shape[:2]
    nb = S // Q_BLOCK
    qs = jnp.moveaxis(q.reshape((B, nb, Q_BLOCK) + q.shape[2:]), 1, 0)
    out = lax.map(block_fn, qs)
    return jnp.moveaxis(out, 0, 1).reshape((B, S) + out.shape[3:])


def softmax_block(qb, k, v, scale):
    s = jnp.einsum('bqngd,bsnd->bngqs', qb, k).astype(jnp.float32) * scale
    p = jax.nn.softmax(s, axis=-1).astype(v.dtype)
    return jnp.einsum('bngqs,bsne->bqnge', p, v)


def diff_block(qb, k, v, lam, scale):
    s = jnp.einsum('bqhmd,bshmd->bhmqs', qb, k).astype(jnp.float32) * scale
    p = jax.nn.softmax(s, axis=-1)
    a = p[:, :, 0] - lam * p[:, :, 1]
    return jnp.einsum('bhqs,bshe->bqhe', a.astype(v.dtype), v)


def two_stream_attention(block_fn, q_l, k_l, v_l, q_c, k_c, v_c, need_ctx):
    k_all = jnp.concatenate([k_l, k_c], axis=1)
    v_all = jnp.concatenate([v_l, v_c], axis=1)
    o_l = sweep_queries(lambda qb: block_fn(qb, k_all, v_all), q_l)
    o_c = sweep_queries(lambda qb: block_fn(qb, k_c, v_c), q_c) if need_ctx else None
    return o_l, o_c


def flat_heads(o):
    return o.reshape(o.shape[:2] + (-1,))


def gqa_branch(pl, pc, qk_gain, rope, need_ctx):
    def prep(q, k, v, rope_tab):
        B, S = q.shape[:2]
        q = rms_norm(q.reshape(B, S, GQA_KV_HEADS, GQA_HEADS // GQA_KV_HEADS, GQA_HEAD_DIM), qk_gain[0])
        k = rms_norm(k.reshape(B, S, GQA_KV_HEADS, GQA_HEAD_DIM), qk_gain[1])
        v = v.reshape(B, S, GQA_KV_HEADS, GQA_HEAD_DIM)
        if rope_tab is not None:
            q = apply_rope(q, rope_tab)
            k = apply_rope(k, rope_tab)
        return q, k, v
    block = functools.partial(softmax_block, scale=GQA_HEAD_DIM ** -0.5)
    o_l, o_c = two_stream_attention(block, *prep(*pl, rope), *prep(*pc, None), need_ctx)
    return flat_heads(o_l), (flat_heads(o_c) if need_ctx else None)


def retention_chunkwise(q, k, v, log_gamma, state0):
    out_dtype = v.dtype
    q, k, v = (a.astype(jnp.float32) for a in (q, k, v))
    log_gamma = log_gamma.astype(jnp.float32)
    B, N, H, _ = q.shape
    nc = N // RET_CHUNK
    idx = jnp.arange(RET_CHUNK, dtype=jnp.float32)
    rel = idx[:, None] - idx[None, :]
    intra = jnp.where(rel >= 0, jnp.exp(log_gamma[:, None, None] * jnp.maximum(rel, 0.0)), 0.0)
    q_decay = jnp.exp(log_gamma[None, :] * (idx[:, None] + 1.0))
    k_decay = jnp.exp(log_gamma[None, :] * (RET_CHUNK - 1.0 - idx[:, None]))
    chunk_decay = jnp.exp(log_gamma * RET_CHUNK)
    to_chunks = lambda a: jnp.moveaxis(a.reshape((B, nc, RET_CHUNK) + a.shape[2:]), 1, 0)

    def step(state, qkv):
        qc, kc, vc = qkv
        scores = jnp.einsum('bihd,bjhd->bhij', qc, kc) * intra
        o = jnp.einsum('bhij,bjhe->bihe', scores, vc)
        o = o + jnp.einsum('bihd,bhde->bihe', qc, state) * q_decay[None, :, :, None]
        state = state * chunk_decay[None, :, None, None] + jnp.einsum(
            'bjhd,bjhe->bhde', kc * k_decay[None, :, :, None], vc)
        return state, o

    state, out = lax.scan(step, state0, (to_chunks(q), to_chunks(k), to_chunks(v)))
    out = jnp.moveaxis(out, 0, 1).reshape(B, N, H, -1)
    return out.astype(out_dtype), state


def retention_branch(pl, pc, log_decay, rope, need_ctx):
    def prep(q, k, v, rope_tab):
        B, S = q.shape[:2]
        q = q.reshape(B, S, RET_HEADS, RET_QK_DIM)
        k = k.reshape(B, S, RET_HEADS, RET_QK_DIM) * (RET_QK_DIM ** -0.5)
        v = v.reshape(B, S, RET_HEADS, RET_V_DIM)
        if rope_tab is not None:
            q = apply_rope(q, rope_tab)
            k = apply_rope(k, rope_tab)
        return q, k, v
    ql, kl, vl = prep(*pl[:3], rope)
    qc, kc, vc = prep(*pc[:3], None)
    flip = lambda a: jnp.flip(a, axis=1)
    zero = jnp.zeros((ql.shape[0], RET_HEADS, RET_QK_DIM, RET_V_DIM), jnp.float32)
    oc_f, s_f = retention_chunkwise(qc, kc, vc, log_decay[0], zero)
    oc_b, s_b = retention_chunkwise(flip(qc), flip(kc), flip(vc), log_decay[1], zero)
    ol_f, _ = retention_chunkwise(ql, kl, vl, log_decay[0], s_f)
    ol_b, _ = retention_chunkwise(flip(ql), flip(kl), flip(vl), log_decay[1], s_b)

    def finish(o, g):
        return flat_heads(rms_norm(o)) * jax.nn.silu(g)
    o_l = finish(ol_f + flip(ol_b), pl[3])
    o_c = finish(oc_f + flip(oc_b), pc[3]) if need_ctx else None
    return o_l, o_c


def mla_branch(pl, pc, cq_gain, ckv_gain, w_uq, w_ukv, qk_gain, rope, need_ctx):
    nope = MLA_NOPE_DIM

    def prep(c_q, c_kv, k_rope, rope_tab):
        B, S = c_q.shape[:2]
        q = (rms_norm(c_q, cq_gain) @ w_uq).reshape(B, S, MLA_HEADS, MLA_NOPE_DIM + MLA_ROPE_DIM)
        kv = (rms_norm(c_kv, ckv_gain) @ w_ukv).reshape(B, S, MLA_HEADS, MLA_NOPE_DIM + MLA_V_DIM)
        q_nope = rms_norm(q[..., :nope], qk_gain[0, :nope])
        q_rope = rms_norm(q[..., nope:], qk_gain[0, nope:])
        k_nope = rms_norm(kv[..., :nope], qk_gain[1, :nope])
        v = kv[..., nope:]
        k_rope = rms_norm(k_rope, qk_gain[1, nope:])
        if rope_tab is not None:
            q_rope = apply_rope(q_rope, rope_tab)
            k_rope = apply_rope(k_rope, rope_tab)
        k_rope = jnp.broadcast_to(k_rope[:, :, None], (B, S, MLA_HEADS, MLA_ROPE_DIM))
        q = jnp.concatenate([q_nope, q_rope], axis=-1)[:, :, :, None]
        k = jnp.concatenate([k_nope, k_rope], axis=-1)
        return q, k, v
    block = functools.partial(softmax_block, scale=(MLA_NOPE_DIM + MLA_ROPE_DIM) ** -0.5)
    o_l, o_c = two_stream_attention(block, *prep(*pl, rope), *prep(*pc, None), need_ctx)
    return flat_heads(o_l), (flat_heads(o_c) if need_ctx else None)


def diff_branch(pl, pc, qk_gain, lam_vecs, subln_gain, lam_init, rope, need_ctx):
    def prep(q, k, v, rope_tab):
        B, S = q.shape[:2]
        q = rms_norm(q.reshape(B, S, DIFF_HEADS, 2, DIFF_HEAD_DIM), qk_gain[0])
        k = rms_norm(k.reshape(B, S, DIFF_HEADS, 2, DIFF_HEAD_DIM), qk_gain[1])
        v = v.reshape(B, S, DIFF_HEADS, 2 * DIFF_HEAD_DIM)
        if rope_tab is not None:
            q = apply_rope(q, rope_tab)
            k = apply_rope(k, rope_tab)
        return q, k, v
    lv = lam_vecs.astype(jnp.float32)
    lam = jnp.exp(jnp.sum(lv[0] * lv[1])) - jnp.exp(jnp.sum(lv[2] * lv[3])) + lam_init
    block = functools.partial(diff_block, lam=lam, scale=DIFF_HEAD_DIM ** -0.5)
    o_l, o_c = two_stream_attention(block, *prep(*pl, rope), *prep(*pc, None), need_ctx)

    def finish(o):
        return flat_heads(rms_norm(o, subln_gain) * (1.0 - lam_init))
    return finish(o_l), (finish(o_c) if need_ctx else None)


def gated_merge(branches, gate_logits, w_branch, w_out):
    stacked = jnp.stack(branches, axis=-2)
    proj = jnp.einsum('bsnw,nwd->bsnd', stacked, w_branch)
    gates = jax.nn.sigmoid(gate_logits.reshape(proj.shape))
    return jnp.sum(gates * proj, axis=-2) @ w_out


def adaln(cond, w_mod, b_mod):
    return (jax.nn.silu(cond) @ w_mod + b_mod).reshape(cond.shape[0], 6, -1)


def modulate(x, shift, scale):
    return x * (1.0 + scale[:, None]) + shift[:, None]


def swiglu(h, w_in, w_out):
    gate, up = jnp.split(h @ w_in, 2, axis=-1)
    return (jax.nn.silu(gate) * up) @ w_out


def setup_inputs(seed: int = 0) -> dict:
    key = jax.random.key(seed)
    ks = jax.random.split(key, 24)
    f32 = jnp.float32
    nrm = lambda k, shape: jax.random.normal(k, shape, f32)
    dense = lambda k, shape, fan_in: nrm(k, shape) * (fan_in ** -0.5)
    gain = lambda k, shape: 1.0 + 0.01 * nrm(k, shape)
    base_decay = jnp.log(2.0 ** (5.0 + jnp.arange(RET_HEADS, dtype=f32)) - 1.0)
    return {
        'x': nrm(ks[0], (BATCH, SEQ, D_MODEL)),
        'c': nrm(ks[1], (BATCH, D_MODEL)),
        'ctx': nrm(ks[2], (BATCH, CTX_LEN, D_MODEL)),
        'c_ctx': nrm(ks[3], (D_MODEL,)),
        'w_mod': dense(ks[4], (DEPTH, D_MODEL, 6 * D_MODEL), D_MODEL),
        'b_mod': 0.01 * nrm(ks[5], (DEPTH, 6 * D_MODEL)),
        'norm_gain': gain(ks[6], (DEPTH, 2, D_MODEL)),
        'w_in': dense(ks[7], (DEPTH, D_MODEL, IN_WIDTH), D_MODEL),
        'gqa_qk_gain': gain(ks[8], (DEPTH, 2, GQA_HEAD_DIM)),
        'ret_decay': base_decay + 0.01 * nrm(ks[9], (DEPTH, 2, RET_HEADS)),
        'mla_cq_gain': gain(ks[10], (DEPTH, MLA_Q_RANK)),
        'mla_ckv_gain': gain(ks[11], (DEPTH, MLA_KV_RANK)),
        'mla_w_uq': dense(ks[12], (DEPTH, MLA_Q_RANK, MLA_HEADS * (MLA_NOPE_DIM + MLA_ROPE_DIM)), MLA_Q_RANK),
        'mla_w_ukv': dense(ks[13], (DEPTH, MLA_KV_RANK, MLA_HEADS * (MLA_NOPE_DIM + MLA_V_DIM)), MLA_KV_RANK),
        'mla_qk_gain': gain(ks[14], (DEPTH, 2, MLA_NOPE_DIM + MLA_ROPE_DIM)),
        'diff_qk_gain': gain(ks[15], (DEPTH, 2, DIFF_HEAD_DIM)),
        'diff_lambda': 0.1 * nrm(ks[16], (DEPTH, 4, DIFF_HEAD_DIM)),
        'diff_subln_gain': gain(ks[17], (DEPTH, 2 * DIFF_HEAD_DIM)),
        'w_branch': dense(ks[18], (DEPTH, N_BRANCHES, BRANCH_WIDTH, D_MODEL), BRANCH_WIDTH),
        'w_out': dense(ks[19], (DEPTH, D_MODEL, D_MODEL), D_MODEL),
        'w_ffn_in': dense(ks[20], (DEPTH, D_MODEL, 2 * FFN_HIDDEN), D_MODEL),
        'w_ffn_out': dense(ks[21], (DEPTH, FFN_HIDDEN, D_MODEL), FFN_HIDDEN),
    }


def reference(x, c, ctx, c_ctx, w_mod, b_mod, norm_gain, w_in, gqa_qk_gain, ret_decay,
              mla_cq_gain, mla_ckv_gain, mla_w_uq, mla_w_ukv, mla_qk_gain, diff_qk_gain,
              diff_lambda, diff_subln_gain, w_branch, w_out, w_ffn_in, w_ffn_out):
    S = x.shape[1]
    ROWS = S // GRID_W
    rope64 = axial_rope(ROWS, 64)
    rope32 = axial_rope(ROWS, MLA_ROPE_DIM)
    for l in range(DEPTH):
        need_ctx = l < DEPTH - 1
        lam_init = 0.8 - 0.6 * math.exp(-0.3 * l)
        mod = adaln(c, w_mod[l], b_mod[l])
        mod_c = adaln(c_ctx[None], w_mod[l], b_mod[l])

        h = modulate(rms_norm(x, norm_gain[l, 0]), mod[:, 0], mod[:, 1])
        hc = modulate(rms_norm(ctx, norm_gain[l, 0]), mod_c[:, 0], mod_c[:, 1])
        pl = split_cols(h @ w_in[l])
        pc = split_cols(hc @ w_in[l])
        a_l, a_c = gqa_branch(pl[0:3], pc[0:3], gqa_qk_gain[l], rope64, need_ctx)
        b_l, b_c = retention_branch(pl[3:7], pc[3:7], jax.nn.log_sigmoid(ret_decay[l].astype(jnp.float32)),
                                    rope64, need_ctx)
        c_l, c_c = mla_branch(pl[7:10], pc[7:10], mla_cq_gain[l], mla_ckv_gain[l], mla_w_uq[l],
                              mla_w_ukv[l], mla_qk_gain[l], rope32, need_ctx)
        d_l, d_c = diff_branch(pl[10:13], pc[10:13], diff_qk_gain[l], diff_lambda[l], diff_subln_gain[l],
                               lam_init, rope64, need_ctx)
        y = gated_merge([a_l, b_l, c_l, d_l], pl[13], w_branch[l], w_out[l])
        x = x + mod[:, 2][:, None] * y

        h2 = modulate(rms_norm(x, norm_gain[l, 1]), mod[:, 3], mod[:, 4])
        x = x + mod[:, 5][:, None] * swiglu(h2, w_ffn_in[l], w_ffn_out[l])

        if need_ctx:
            yc = gated_merge([a_c, b_c, c_c, d_c], pc[13], w_branch[l], w_out[l])
            ctx = ctx + mod_c[:, 2][:, None] * yc
            hc2 = modulate(rms_norm(ctx, norm_gain[l, 1]), mod_c[:, 3], mod_c[:, 4])
            ctx = ctx + mod_c[:, 5][:, None] * swiglu(hc2, w_ffn_in[l], w_ffn_out[l])
    return x
```

```python
import functools
import math

import jax
import jax.numpy as jnp
from jax import lax
from jax.experimental import pallas as pl
from jax.experimental.pallas import tpu as pltpu

F32 = jnp.float32
BF16 = jnp.bfloat16

D_MODEL = 1024
GRID_W = 64
ROPE_THETA = 10000.0
EPS = 1e-6
RET_CHUNK = 128
N_HEADS_RET = 4
FFN_HIDDEN = 2816
LANES = 128
ROW_TILE = 256
KV_CHUNK = 512
VMEM_LIMIT = 56 * 1024 * 1024

SEG = dict(a_q=(0, 512), a_k=(512, 640), a_v=(640, 768),
           b_q=(768, 1024), b_k=(1024, 1280), b_v=(1280, 1792), b_g=(1792, 2304),
           c_q=(2304, 2688), c_kv=(2688, 2944), c_kr=(2944, 3072),
           d_q=(3072, 3584), d_k=(3584, 4096), d_v=(4096, 4608))
MAIN_WIDTH = 4608


def _params(sem):
    return pltpu.CompilerParams(dimension_semantics=sem, vmem_limit_bytes=VMEM_LIMIT)


def _const_spec(shape):
    n = len(shape)
    return pl.BlockSpec(shape, lambda *_: (0,) * n, pipeline_mode=pl.Buffered(1))


def _sigmoid(x):
    return 1.0 / (1.0 + jnp.exp(-x))


def _silu(x):
    return x * _sigmoid(x)


def _rms(x, n):
    ss = jnp.sum(x * x, axis=-1, keepdims=True) * (1.0 / n)
    return x * lax.rsqrt(ss + EPS)


def _lane(shape):
    return lax.broadcasted_iota(jnp.int32, shape, 1)


def _rope(y, cos, sin_signed):
    even = (_lane(y.shape) & 1) == 0
    partner = jnp.where(even, pltpu.roll(y, LANES - 1, 1), pltpu.roll(y, 1, 1))
    return y * cos + partner * sin_signed


def _head_slabs(p):
    lo = _lane((p.shape[0], LANES)) < 64
    out = []
    for j in range(p.shape[1] // LANES):
        src = p[:, j * LANES:(j + 1) * LANES]
        out.append(jnp.where(lo, src, 0.0))
        out.append(jnp.where(lo, pltpu.roll(src, 64, 1), 0.0))
    return out


def _dot(a, b):
    return jnp.dot(a, b, preferred_element_type=F32)


def _mod_kernel(c_ref, w_ref, b_ref, o_ref):
    s = _silu(c_ref[...]).astype(BF16)
    o_ref[0] = _dot(s, w_ref[0].astype(BF16)) + b_ref[0]


def _mod_call(cc, w_mod, b_mod):
    depth = w_mod.shape[0]
    nt = 6
    return pl.pallas_call(
        _mod_kernel,
        grid=(depth, nt),
        in_specs=[pl.BlockSpec((8, D_MODEL), lambda l, j: (0, 0)),
                  pl.BlockSpec((1, D_MODEL, D_MODEL), lambda l, j: (l, 0, j)),
                  pl.BlockSpec((1, 1, D_MODEL), lambda l, j: (l, 0, j))],
        out_specs=pl.BlockSpec((1, 8, D_MODEL), lambda l, j: (l, 0, j)),
        out_shape=jax.ShapeDtypeStruct((depth, 8, 6 * D_MODEL), F32),
        compiler_params=_params(("arbitrary", "arbitrary")),
        name="mod",
    )(cc, w_mod, b_mod.reshape(depth, 1, 6 * D_MODEL))


def _prenorm(x, gain, mod, shift_row):
    h = _rms(x, D_MODEL) * gain
    return h * (1.0 + mod[shift_row + 1:shift_row + 2, :]) + mod[shift_row:shift_row + 1, :]


def _prep_kernel(x_ref, mod_ref, gain_ref, w_ref, wuq_ref, wuk_ref, wuv_ref,
                 hg_ref, cqg_ref, ckvg_ref, c64_ref, s64_ref, cm_ref, sm_ref,
                 qa_ref, ka_ref, va_ref, qb_ref, kb_ref, vb_ref, gb_ref,
                 qc_ref, kc_ref, vc_ref, qd_ref, kd_ref, vd_ref):
    hb = _prenorm(x_ref[...], gain_ref[...], mod_ref[0], 0).astype(BF16)
    t = hb.shape[0]
    lane = _lane((t, LANES))
    c64, s64 = c64_ref[...], s64_ref[...]
    cm, sm = cm_ref[...], sm_ref[...]

    def proj(name):
        a, b = SEG[name]
        return _dot(hb, w_ref[:, a:b])

    def put(ref, j, val):
        ref[:, j * LANES:(j + 1) * LANES] = val.astype(ref.dtype)

    def rms64(slab):

        ss = jnp.sum(slab * slab, axis=-1, keepdims=True) * (1.0 / 64.0)
        return slab * lax.rsqrt(ss + EPS)

    def nr64(slab, gain_row, scale):
        y = rms64(slab) * hg_ref[gain_row:gain_row + 1, :]
        y = _rope(y, c64, s64)
        return y * scale if scale != 1.0 else y

    ones_at_64 = lane == 64

    for j, slab in enumerate(_head_slabs(proj("a_q"))):
        put(qa_ref, j, nr64(slab, 0, 64 ** -0.5))
    for j, slab in enumerate(_head_slabs(proj("a_k"))):
        put(ka_ref, j, nr64(slab, 1, 1.0))
    for j, slab in enumerate(_head_slabs(proj("a_v"))):
        put(va_ref, j, jnp.where(ones_at_64, 1.0, slab))

    q = proj("b_q")
    k = proj("b_k") * (64 ** -0.5)
    for j in range(2):
        put(qb_ref, j, _rope(q[:, j * LANES:(j + 1) * LANES], c64, s64))
        put(kb_ref, j, _rope(k[:, j * LANES:(j + 1) * LANES], c64, s64))
    vb_ref[...] = proj("b_v").astype(vb_ref.dtype)
    gb_ref[...] = proj("b_g")

    nope = lane < 64
    ropel = (lane >= 64) & (lane < 96)
    cq = (_rms(proj("c_q"), 384.0) * cqg_ref[...]).astype(BF16)
    qm = _dot(cq, wuq_ref[...])
    gq = hg_ref[2:3, :]
    for h in range(8):
        slab = qm[:, h * LANES:(h + 1) * LANES]
        sq = slab * slab
        ssn = jnp.sum(jnp.where(nope, sq, 0.0), axis=-1, keepdims=True) * (1.0 / 64.0)
        ssr = jnp.sum(jnp.where(ropel, sq, 0.0), axis=-1, keepdims=True) * (1.0 / 32.0)
        inv = jnp.where(nope, lax.rsqrt(ssn + EPS), lax.rsqrt(ssr + EPS))
        y = _rope(slab * inv * gq, cm, sm) * (96 ** -0.5)
        put(qc_ref, h, y)
    ckv = (_rms(proj("c_kv"), 256.0) * ckvg_ref[...]).astype(BF16)
    kn = _dot(ckv, wuk_ref[...])
    vm = _dot(ckv, wuv_ref[...])
    gk = hg_ref[3:4, :]
    kr = pltpu.roll(proj("c_kr"), 64, 1)
    ssk = jnp.sum(kr * kr, axis=-1, keepdims=True) * (1.0 / 32.0)
    kr = _rope(kr * lax.rsqrt(ssk + EPS) * gk, cm, sm)
    kr = jnp.where(ropel, kr, 0.0)
    for h in range(8):
        slab = kn[:, h * LANES:(h + 1) * LANES]
        put(kc_ref, h, jnp.where(nope, rms64(slab) * gk, kr))
        put(vc_ref, h, jnp.where(ones_at_64, 1.0, vm[:, h * LANES:(h + 1) * LANES]))

    for j, slab in enumerate(_head_slabs(proj("d_q"))):
        put(qd_ref, j, nr64(slab, 4, 64 ** -0.5))
    for j, slab in enumerate(_head_slabs(proj("d_k"))):
        put(kd_ref, j, nr64(slab, 5, 1.0))
    v = proj("d_v")
    ones_at_0 = jnp.where(lane == 0, 1.0, 0.0)
    for h in range(4):
        put(vd_ref, 2 * h, v[:, h * LANES:(h + 1) * LANES])
        put(vd_ref, 2 * h + 1, ones_at_0)


def _prep_call(xall, modsel, gain, wl, tabs):
    r = xall.shape[0]
    nt = r // ROW_TILE
    row = lambda w: pl.BlockSpec((ROW_TILE, w), lambda i: (i, 0))
    out_w = dict(qa=1024, ka=256, va=256, qb=256, kb=256, vb=512, gb=512,
                 qc=1024, kc=1024, vc=1024, qd=1024, kd=1024, vd=1024)
    out_shape = [jax.ShapeDtypeStruct((r, w), F32 if n == "gb" else BF16) for n, w in out_w.items()]
    outs = pl.pallas_call(
        _prep_kernel,
        grid=(nt,),
        in_specs=[row(D_MODEL),
                  pl.BlockSpec((1, 8, D_MODEL), lambda i: (jnp.minimum(i, 1), 0, 0)),
                  _const_spec((1, D_MODEL)),
                  _const_spec((D_MODEL, MAIN_WIDTH)),
                  _const_spec((384, 1024)), _const_spec((256, 1024)), _const_spec((256, 1024)),
                  _const_spec((8, LANES)), _const_spec((1, 384)), _const_spec((1, 256)),
                  row(LANES), row(LANES), row(LANES), row(LANES)],
        out_specs=[row(w) for w in out_w.values()],
        out_shape=out_shape,
        compiler_params=_params(("arbitrary",)),
        name="prep",
    )(xall, modsel, gain, wl["w_main"], wl["wuq"], wl["wuk"], wl["wuv"],
      wl["head_gain"], wl["cq_gain"], wl["ckv_gain"], *tabs)
    return dict(zip(out_w.keys(), outs))


def _ret_kernel(dec_ref, qf_ref, kf_ref, vf_ref, qr_ref, kr_ref, vr_ref,
                of_ref, or_ref, state_ref, intra_ref, qdec_ref, kdec_ref, cdec_ref):
    c = RET_CHUNK

    @pl.when(pl.program_id(0) == 0)
    def _init():
        x = dec_ref[...]
        lg = jnp.minimum(x, 0.0) - jnp.log(1.0 + jnp.exp(-jnp.abs(x)))
        i_idx = lax.broadcasted_iota(jnp.int32, (c, c), 0).astype(F32)
        j_idx = lax.broadcasted_iota(jnp.int32, (c, c), 1).astype(F32)
        for d in range(2):
            for h in range(N_HEADS_RET):
                g = lg[d:d + 1, h:h + 1]
                if d == 0:
                    rel, qe, ke = i_idx - j_idx, i_idx + 1.0, c - 1.0 - i_idx
                else:
                    rel, qe, ke = j_idx - i_idx, c - i_idx, i_idx
                intra_ref[d, h] = jnp.where(rel >= 0, jnp.exp(g * jnp.maximum(rel, 0.0)), 0.0)
                qdec_ref[d, h] = jnp.exp(g * qe)
                kdec_ref[d, h] = jnp.exp(g * ke[:, :64])
                cdec_ref[d, h] = jnp.exp(jnp.broadcast_to(g, (8, LANES)) * float(c))
        state_ref[...] = jnp.zeros_like(state_ref)

    for d, (q_ref, k_ref, v_ref, o_ref) in enumerate(
            ((qf_ref, kf_ref, vf_ref, of_ref), (qr_ref, kr_ref, vr_ref, or_ref))):
        for h in range(N_HEADS_RET):
            qh = q_ref[:, h * 64:(h + 1) * 64]
            kh = k_ref[:, h * 64:(h + 1) * 64]
            vh = v_ref[:, h * LANES:(h + 1) * LANES]
            sc = lax.dot_general(qh, kh, (((1,), (1,)), ((), ())), preferred_element_type=F32)
            sc = sc * intra_ref[d, h]
            st = state_ref[d, h]
            o = _dot(sc.astype(BF16), vh) + _dot(qh, st.astype(BF16)) * qdec_ref[d, h]
            o_ref[:, h * LANES:(h + 1) * LANES] = o
            kd = (kh.astype(F32) * kdec_ref[d, h]).astype(BF16)
            upd = lax.dot_general(kd, vh, (((0,), (0,)), ((), ())), preferred_element_type=F32)
            state_ref[d, h] = st * cdec_ref[d, h][0:1, :] + upd


def _ret_call(qb, kb, vb, dec):
    r = qb.shape[0]
    nc = r // RET_CHUNK
    nctx = ROW_TILE // RET_CHUNK
    fwd = lambda t: (t, 0)
    bwd = lambda t: (jnp.where(t < nctx, nctx - 1 - t, nc + nctx - 1 - t), 0)
    blk = lambda w, m: pl.BlockSpec((RET_CHUNK, w), m)
    return pl.pallas_call(
        _ret_kernel,
        grid=(nc,),
        in_specs=[_const_spec((8, LANES)),
                  blk(256, fwd), blk(256, fwd), blk(512, fwd),
                  blk(256, bwd), blk(256, bwd), blk(512, bwd)],
        out_specs=[blk(512, fwd), blk(512, bwd)],
        out_shape=[jax.ShapeDtypeStruct((r, 512), F32)] * 2,
        scratch_shapes=[pltpu.VMEM((2, N_HEADS_RET, 64, LANES), F32),
                        pltpu.VMEM((2, N_HEADS_RET, RET_CHUNK, RET_CHUNK), F32),
                        pltpu.VMEM((2, N_HEADS_RET, RET_CHUNK, RET_CHUNK), F32),
                        pltpu.VMEM((2, N_HEADS_RET, RET_CHUNK, 64), F32),
                        pltpu.VMEM((2, N_HEADS_RET, 8, LANES), F32)],
        compiler_params=_params(("arbitrary",)),
        name="retention",
    )(dec, qb, kb, vb, qb, kb, vb)


def _attn_kernel(*refs, nm, dv, n_ctx, n_chunks, diff_scale):
    if diff_scale is None:
        q_ref, k_ref, v_ref, o_ref, m_ref, acc_ref = refs
    else:
        q_ref, k_ref, v_ref, lam_ref, sg_ref, o_ref, m_ref, acc_ref = refs
    i = pl.program_id(1)
    m_ref[...] = jnp.full_like(m_ref, -1e30)
    acc_ref[...] = jnp.zeros_like(acc_ref)

    def step(off, size):
        v = v_ref[pl.ds(off, size), :]
        for a in range(nm):
            q = q_ref[:, a * LANES:(a + 1) * LANES]
            k = k_ref[pl.ds(off, size), a * LANES:(a + 1) * LANES]
            s = lax.dot_general(q, k, (((1,), (1,)), ((), ())), preferred_element_type=F32)
            m_old = m_ref[a]
            m_new = jnp.maximum(m_old, jnp.max(s, axis=-1, keepdims=True))
            p = jnp.exp(s - m_new)
            acc_ref[a] = acc_ref[a] * jnp.exp(m_old - m_new) + _dot(p.astype(BF16), v)
            m_ref[a] = m_new

    @pl.when(i > 0)
    def _latent_keys():
        def body(j, carry):
            step(pl.multiple_of(n_ctx + j * KV_CHUNK, 256), KV_CHUNK)
            return carry
        lax.fori_loop(0, n_chunks, body, 0)

    step(0, n_ctx)

    def normalised(a):
        acc = acc_ref[a]
        return acc * (1.0 / acc[:, dv:dv + 1])

    if diff_scale is None:
        o_ref[...] = normalised(0).astype(o_ref.dtype)
    else:
        lv = lam_ref[...]
        lam_init = 1.0 - diff_scale
        lam = (jnp.exp(jnp.sum(lv[0:1] * lv[1:2], axis=-1, keepdims=True))
               - jnp.exp(jnp.sum(lv[2:3] * lv[3:4], axis=-1, keepdims=True)) + lam_init)
        o = normalised(0)[:, :dv] - lam * normalised(1)[:, :dv]
        o_ref[...] = (_rms(o, float(dv)) * sg_ref[...] * diff_scale).astype(o_ref.dtype)


def _attn_call(q, k, v, *, n_heads, kv_of, nm, dv, name, extra=(), diff_scale=None):
    r = q.shape[0]
    nq = r // ROW_TILE
    dvp = v.shape[1] // (kv_of(n_heads - 1) + 1)
    n_chunks = (r - ROW_TILE) // KV_CHUNK
    kern = functools.partial(_attn_kernel, nm=nm, dv=dv, n_ctx=ROW_TILE, n_chunks=n_chunks,
                             diff_scale=diff_scale)
    extra_specs = [pl.BlockSpec(e.shape, lambda h, i: (0, 0)) for e in extra]
    return pl.pallas_call(
        kern,
        grid=(n_heads, nq),
        in_specs=[pl.BlockSpec((ROW_TILE, nm * LANES), lambda h, i: (i, h)),
                  pl.BlockSpec((r, nm * LANES), lambda h, i: (0, kv_of(h))),
                  pl.BlockSpec((r, dvp), lambda h, i: (0, kv_of(h)))] + extra_specs,
        out_specs=pl.BlockSpec((ROW_TILE, LANES), lambda h, i: (i, h)),
        out_shape=jax.ShapeDtypeStruct((r, n_heads * LANES), BF16),
        scratch_shapes=[pltpu.VMEM((nm, ROW_TILE, 1), F32),
                        pltpu.VMEM((nm, ROW_TILE, dvp), F32)],
        compiler_params=_params(("arbitrary", "arbitrary")),
        name=name,
    )(q, k, v, *extra)


def _merge_kernel(x_ref, mod_ref, gain_ref, wg_ref, oa_ref, of_ref, or_ref, gb_ref, oc_ref, od_ref,
                  wa_ref, wb_ref, wc_ref, wd_ref, wo_ref, out_ref):
    x = x_ref[...]
    mod = mod_ref[0]
    hb = _prenorm(x, gain_ref[...], mod, 0).astype(BF16)
    ob = of_ref[...] + or_ref[...]
    g = gb_ref[...]
    parts = [_rms(ob[:, h * LANES:(h + 1) * LANES], float(LANES)) for h in range(N_HEADS_RET)]
    bb = (jnp.concatenate(parts, axis=-1) * _silu(g)).astype(BF16)
    branches = ((oa_ref[...], wa_ref), (bb, wb_ref), (oc_ref[...], wc_ref), (od_ref[...], wd_ref))
    acc = None
    for n, (b, w_ref) in enumerate(branches):
        gate = _sigmoid(_dot(hb, wg_ref[:, n * D_MODEL:(n + 1) * D_MODEL]))
        term = gate * _dot(b, w_ref[...])
        acc = term if acc is None else acc + term
    y = _dot(acc.astype(BF16), wo_ref[...])
    out_ref[...] = x + mod[2:3, :] * y


def _merge_call(xall, modsel, gain, wl, oa, of, orv, gb, oc, od, *, skip_ctx):
    r = xall.shape[0]
    off = 1 if skip_ctx else 0
    nt = r // ROW_TILE - off
    row = lambda w: pl.BlockSpec((ROW_TILE, w), lambda i: (i + off, 0))
    return pl.pallas_call(
        _merge_kernel,
        grid=(nt,),
        in_specs=[row(D_MODEL),
                  pl.BlockSpec((1, 8, D_MODEL), lambda i: (jnp.minimum(i + off, 1), 0, 0)),
                  _const_spec((1, D_MODEL)),
                  _const_spec((D_MODEL, 4 * D_MODEL)),
                  row(1024), row(512), row(512), row(512), row(1024), row(512),
                  _const_spec((1024, D_MODEL)), _const_spec((512, D_MODEL)),
                  _const_spec((1024, D_MODEL)), _const_spec((512, D_MODEL)),
                  _const_spec((D_MODEL, D_MODEL))],
        out_specs=pl.BlockSpec((ROW_TILE, D_MODEL), lambda i: (i, 0)),
        out_shape=jax.ShapeDtypeStruct((nt * ROW_TILE, D_MODEL), F32),
        compiler_params=_params(("arbitrary",)),
        name="merge",
    )(xall, modsel, gain, wl["w_gate"], oa, of, orv, gb, oc, od,
      wl["wb_a"], wl["wb_b"], wl["wb_c"], wl["wb_d"], wl["w_out"])


def _ffn_kernel(x_ref, mod_ref, gain_ref, wi_ref, wo_ref, out_ref):
    x = x_ref[...]
    mod = mod_ref[0]
    hb = _prenorm(x, gain_ref[...], mod, 3).astype(BF16)
    gate = _dot(hb, wi_ref[:, :FFN_HIDDEN])
    up = _dot(hb, wi_ref[:, FFN_HIDDEN:])
    y = _dot((_silu(gate) * up).astype(BF16), wo_ref[...])
    out_ref[...] = x + mod[5:6, :] * y


def _ffn_call(x1, modsel, gain, wl, *, has_ctx):
    r = x1.shape[0]
    nt = r // ROW_TILE
    off = 0 if has_ctx else 1
    return pl.pallas_call(
        _ffn_kernel,
        grid=(nt,),
        in_specs=[pl.BlockSpec((ROW_TILE, D_MODEL), lambda i: (i, 0)),
                  pl.BlockSpec((1, 8, D_MODEL), lambda i: (jnp.minimum(i + off, 1), 0, 0)),
                  _const_spec((1, D_MODEL)),
                  _const_spec((D_MODEL, 2 * FFN_HIDDEN)),
                  _const_spec((FFN_HIDDEN, D_MODEL))],
        out_specs=pl.BlockSpec((ROW_TILE, D_MODEL), lambda i: (i, 0)),
        out_shape=jax.ShapeDtypeStruct((r, D_MODEL), F32),
        compiler_params=_params(("arbitrary",)),
        name="ffn",
    )(x1, modsel, gain, wl["w_ffn_in"], wl["w_ffn_out"])


def _pad_rows_per_head(w, n_heads, width):
    w = w.reshape(n_heads, width, w.shape[-1])
    w = jnp.pad(w, ((0, 0), (0, LANES - width), (0, 0)))
    return w.reshape(n_heads * LANES, -1)


def _pad_cols_per_head(w, n_heads, width):
    w = w.reshape(w.shape[0], n_heads, width)
    w = jnp.pad(w, ((0, 0), (0, 0), (0, LANES - width)))
    return w.reshape(w.shape[0], n_heads * LANES)


def _pad_vec(v):
    return jnp.pad(v, (0, LANES - v.shape[0]))


def _layer_weights(l, w_in, gqa_qk_gain, mla_cq_gain, mla_ckv_gain, mla_w_uq, mla_w_ukv,
                   mla_qk_gain, diff_qk_gain, w_branch, w_out, w_ffn_in, w_ffn_out):
    wi = w_in[l]
    w_main = jnp.concatenate([wi[:, :2976], jnp.zeros((D_MODEL, 96), F32), wi[:, 2976:4512]], axis=1)
    ukv = mla_w_ukv[l].reshape(256, 8, 128)
    head_gain = jnp.stack([
        _pad_vec(gqa_qk_gain[l, 0]), _pad_vec(gqa_qk_gain[l, 1]),
        _pad_vec(mla_qk_gain[l, 0]), _pad_vec(mla_qk_gain[l, 1]),
        _pad_vec(diff_qk_gain[l, 0]), _pad_vec(diff_qk_gain[l, 1]),
        jnp.zeros((LANES,), F32), jnp.zeros((LANES,), F32)])
    wb = w_branch[l]
    return dict(
        w_main=w_main.astype(BF16),
        w_gate=wi[:, 4512:].astype(BF16),
        wuq=_pad_cols_per_head(mla_w_uq[l], 8, 96).astype(BF16),
        wuk=_pad_cols_per_head(ukv[:, :, :64].reshape(256, 512), 8, 64).astype(BF16),
        wuv=_pad_cols_per_head(ukv[:, :, 64:].reshape(256, 512), 8, 64).astype(BF16),
        head_gain=head_gain,
        cq_gain=mla_cq_gain[l][None], ckv_gain=mla_ckv_gain[l][None],
        wb_a=_pad_rows_per_head(wb[0], 8, 64).astype(BF16),
        wb_b=wb[1].astype(BF16),
        wb_c=_pad_rows_per_head(wb[2], 8, 64).astype(BF16),
        wb_d=wb[3].astype(BF16),
        w_out=w_out[l].astype(BF16),
        w_ffn_in=w_ffn_in[l].astype(BF16),
        w_ffn_out=w_ffn_out[l].astype(BF16),
    )


def _rope_tables(n_ctx, seq):
    n_rows = seq // GRID_W
    row = jnp.repeat(jnp.arange(n_rows, dtype=F32), GRID_W)
    col = jnp.tile(jnp.arange(GRID_W, dtype=F32), n_rows)

    def pairs(dim):
        quarter = dim // 4
        freqs = ROPE_THETA ** (-jnp.arange(quarter, dtype=F32) / quarter)
        ang = jnp.concatenate([row[:, None] * freqs, col[:, None] * freqs], axis=-1)
        cos = jnp.repeat(jnp.cos(ang), 2, axis=-1)
        sin = jnp.repeat(jnp.sin(ang), 2, axis=-1) * jnp.tile(jnp.array([-1.0, 1.0], F32), dim // 2)
        return cos, sin

    c64, s64 = pairs(64)
    c32, s32 = pairs(32)
    one = lambda w: jnp.ones((seq, w), F32)
    zero = lambda w: jnp.zeros((seq, w), F32)
    t64c = jnp.concatenate([c64, c64], axis=-1)
    t64s = jnp.concatenate([s64, s64], axis=-1)
    tmc = jnp.concatenate([one(64), c32, one(32)], axis=-1)
    tms = jnp.concatenate([zero(64), s32, zero(32)], axis=-1)
    ctx_c = jnp.ones((n_ctx, LANES), F32)
    ctx_s = jnp.zeros((n_ctx, LANES), F32)
    cat = lambda ctx_t, t: jnp.concatenate([ctx_t, t], axis=0)
    return cat(ctx_c, t64c), cat(ctx_s, t64s), cat(ctx_c, tmc), cat(ctx_s, tms)


def kernel(x, c, ctx, c_ctx, w_mod, b_mod, norm_gain, w_in, gqa_qk_gain, ret_decay, mla_cq_gain,
           mla_ckv_gain, mla_w_uq, mla_w_ukv, mla_qk_gain, diff_qk_gain, diff_lambda, diff_subln_gain,
           w_branch, w_out, w_ffn_in, w_ffn_out):
    depth = w_mod.shape[0]
    batch, seq, _ = x.shape
    n_ctx = ctx.shape[1]
    assert batch == 1 and n_ctx == ROW_TILE and seq % KV_CHUNK == 0

    tabs = _rope_tables(n_ctx, seq)
    cc = jnp.concatenate([c_ctx[None], c, jnp.zeros((6, D_MODEL), F32)], axis=0)
    mods = _mod_call(cc, w_mod, b_mod)
    xall = jnp.concatenate([ctx[0], x[0]], axis=0)

    for l in range(depth):
        last = l == depth - 1
        lam_init = 0.8 - 0.6 * math.exp(-0.3 * l)
        wl = _layer_weights(l, w_in, gqa_qk_gain, mla_cq_gain, mla_ckv_gain, mla_w_uq, mla_w_ukv,
                            mla_qk_gain, diff_qk_gain, w_branch, w_out, w_ffn_in, w_ffn_out)
        modsel = jnp.pad(mods[l, :2].reshape(2, 6, D_MODEL), ((0, 0), (0, 2), (0, 0)))
        g_attn = norm_gain[l, 0][None]
        g_ffn = norm_gain[l, 1][None]

        p = _prep_call(xall, modsel, g_attn, wl, tabs)
        dec = jnp.pad(ret_decay[l].astype(F32), ((0, 6), (0, LANES - N_HEADS_RET)))
        of, orv = _ret_call(p["qb"], p["kb"], p["vb"], dec)
        oa = _attn_call(p["qa"], p["ka"], p["va"], n_heads=8, kv_of=lambda h: h // 4, nm=1, dv=64,
                        name="attn_gqa")
        oc = _attn_call(p["qc"], p["kc"], p["vc"], n_heads=8, kv_of=lambda h: h, nm=1, dv=64,
                        name="attn_mla")
        lam_tab = jnp.pad(diff_lambda[l].astype(F32), ((0, 4), (0, LANES - 64)))
        od = _attn_call(p["qd"], p["kd"], p["vd"], n_heads=4, kv_of=lambda h: h, nm=2, dv=128,
                        name="attn_diff", extra=(lam_tab, diff_subln_gain[l][None]),
                        diff_scale=1.0 - lam_init)
        x1 = _merge_call(xall, modsel, g_attn, wl, oa, of, orv, p["gb"], oc, od, skip_ctx=last)
        xall = _ffn_call(x1, modsel, g_ffn, wl, has_ctx=not last)

    return xall[None] if xall.shape[0] == seq else xall[n_ctx:][None]
```

```python
import functools
import math

import jax
import jax.numpy as jnp
from jax import lax
from jax.experimental import pallas as pl
from jax.experimental.pallas import tpu as pltpu

F32 = jnp.float32
BF16 = jnp.bfloat16

D_MODEL = 1024
GRID_W = 64
ROPE_THETA = 10000.0
EPS = 1e-6
RET_CHUNK = 128
N_HEADS_RET = 4
FFN_HIDDEN = 2816
LANES = 128
BF16_ROWS = 16
ROW_TILE = 256
CTX = 256
ATT_TQ = 512
KV_CHUNK = 512
LOG2E = 1.4426950408889634
VMEM_LIMIT = 56 * 1024 * 1024

SEG = dict(a_q=(0, 512), a_k=(512, 640), a_v=(640, 768),
           b_q=(768, 1024), b_k=(1024, 1280), b_v=(1280, 1792), b_g=(1792, 2304),
           c_q=(2304, 2688), c_kv=(2688, 2944), c_kr=(2944, 3072),
           d_q=(3072, 3584), d_k=(3584, 4096), d_v=(4096, 4608))
MAIN_WIDTH = 4608


def _params(sem):
    return pltpu.CompilerParams(dimension_semantics=sem, vmem_limit_bytes=VMEM_LIMIT)


def _const_spec(shape):
    n = len(shape)
    return pl.BlockSpec(shape, lambda *_: (0,) * n, pipeline_mode=pl.Buffered(1))


def _mod_spec(n_lat_tiles):
    return pl.BlockSpec((1, 8, D_MODEL), lambda i: (jnp.where(i < n_lat_tiles, 1, 0), 0, 0))


def _sigmoid(x):
    return 1.0 / (1.0 + jnp.exp(-x))


def _silu(x):
    return x * _sigmoid(x)


def _rms(x, n):
    ss = jnp.sum(x * x, axis=-1, keepdims=True) * (1.0 / n)
    return x * lax.rsqrt(ss + EPS)


def _lane(shape):
    return lax.broadcasted_iota(jnp.int32, shape, 1)


def _rope(y, cos, sin_signed):
    even = (_lane(y.shape) & 1) == 0
    partner = jnp.where(even, pltpu.roll(y, LANES - 1, 1), pltpu.roll(y, 1, 1))
    return y * cos + partner * sin_signed


def _head_slabs(p):
    lo = _lane((p.shape[0], LANES)) < 64
    out = []
    for j in range(p.shape[1] // LANES):
        src = p[:, j * LANES:(j + 1) * LANES]
        out.append(jnp.where(lo, src, 0.0))
        out.append(jnp.where(lo, pltpu.roll(src, 64, 1), 0.0))
    return out


def _dot(a, b):
    return jnp.dot(a, b, preferred_element_type=F32)


def _mod_kernel(c_ref, w_ref, b_ref, o_ref):
    s = _silu(c_ref[...]).astype(BF16)
    o_ref[0] = _dot(s, w_ref[0].astype(BF16)) + b_ref[0]


def _mod_call(cc, w_mod, b_mod):
    depth = w_mod.shape[0]
    nt = 6
    return pl.pallas_call(
        _mod_kernel,
        grid=(depth, nt),
        in_specs=[pl.BlockSpec((8, D_MODEL), lambda l, j: (0, 0)),
                  pl.BlockSpec((1, D_MODEL, D_MODEL), lambda l, j: (l, 0, j)),
                  pl.BlockSpec((1, 1, D_MODEL), lambda l, j: (l, 0, j))],
        out_specs=pl.BlockSpec((1, 8, D_MODEL), lambda l, j: (l, 0, j)),
        out_shape=jax.ShapeDtypeStruct((depth, 8, 6 * D_MODEL), F32),
        compiler_params=_params(("arbitrary", "arbitrary")),
        name="mod",
    )(cc, w_mod, b_mod.reshape(depth, 1, 6 * D_MODEL))


def _prenorm(x, gain, mod, shift_row):
    h = _rms(x, D_MODEL) * gain
    return h * (1.0 + mod[shift_row + 1:shift_row + 2, :]) + mod[shift_row:shift_row + 1, :]


def _prep_kernel(x_ref, mod_ref, gain_ref, w_ref, wuq_ref, wuk_ref, wuv_ref,
                 hg_ref, cqg_ref, ckvg_ref, c64_ref, s64_ref, cm_ref, sm_ref,
                 qa_ref, ka_ref, va_ref, qb_ref, kb_ref, vb_ref, gb_ref,
                 qc_ref, kc_ref, vc_ref, qd_ref, kd_ref, vd_ref):
    hb = _prenorm(x_ref[...], gain_ref[...], mod_ref[0], 0).astype(BF16)
    t = hb.shape[0]
    lane = _lane((t, LANES))
    c64, s64 = c64_ref[...], s64_ref[...]
    cm, sm = cm_ref[...], sm_ref[...]

    def proj(name):
        a, b = SEG[name]
        return _dot(hb, w_ref[:, a:b])

    def put(ref, j, val):
        ref[:, j * LANES:(j + 1) * LANES] = val.astype(ref.dtype)

    def rms64(slab):
        ss = jnp.sum(slab * slab, axis=-1, keepdims=True) * (1.0 / 64.0)
        return slab * lax.rsqrt(ss + EPS)

    def nr64(slab, gain_row, scale):
        y = rms64(slab) * hg_ref[gain_row:gain_row + 1, :]
        y = _rope(y, c64, s64)
        return y * scale if scale != 1.0 else y

    ones_at_64 = lane == 64

    for j, slab in enumerate(_head_slabs(proj("a_q"))):
        put(qa_ref, j, nr64(slab, 0, 64 ** -0.5 * LOG2E))
    for j, slab in enumerate(_head_slabs(proj("a_k"))):
        put(ka_ref, j, nr64(slab, 1, 1.0))
    for j, slab in enumerate(_head_slabs(proj("a_v"))):
        put(va_ref, j, jnp.where(ones_at_64, 1.0, slab))

    q = proj("b_q")
    k = proj("b_k") * (64 ** -0.5)
    for j in range(2):
        put(qb_ref, j, _rope(q[:, j * LANES:(j + 1) * LANES], c64, s64))
        put(kb_ref, j, _rope(k[:, j * LANES:(j + 1) * LANES], c64, s64))
    vb_ref[...] = proj("b_v").astype(vb_ref.dtype)
    gb_ref[...] = proj("b_g")

    nope = lane < 64
    ropel = (lane >= 64) & (lane < 96)
    cq = (_rms(proj("c_q"), 384.0) * cqg_ref[...]).astype(BF16)
    qm = _dot(cq, wuq_ref[...])
    gq = hg_ref[2:3, :]
    for h in range(8):
        slab = qm[:, h * LANES:(h + 1) * LANES]
        sq = slab * slab
        ssn = jnp.sum(jnp.where(nope, sq, 0.0), axis=-1, keepdims=True) * (1.0 / 64.0)
        ssr = jnp.sum(jnp.where(ropel, sq, 0.0), axis=-1, keepdims=True) * (1.0 / 32.0)
        inv = jnp.where(nope, lax.rsqrt(ssn + EPS), lax.rsqrt(ssr + EPS))
        y = _rope(slab * inv * gq, cm, sm) * (96 ** -0.5 * LOG2E)
        put(qc_ref, h, y)
    ckv = (_rms(proj("c_kv"), 256.0) * ckvg_ref[...]).astype(BF16)
    kn = _dot(ckv, wuk_ref[...])
    vm = _dot(ckv, wuv_ref[...])
    gk = hg_ref[3:4, :]
    kr = pltpu.roll(proj("c_kr"), 64, 1)
    ssk = jnp.sum(kr * kr, axis=-1, keepdims=True) * (1.0 / 32.0)
    kr = _rope(kr * lax.rsqrt(ssk + EPS) * gk, cm, sm)
    kr = jnp.where(ropel, kr, 0.0)
    for h in range(8):
        slab = kn[:, h * LANES:(h + 1) * LANES]
        put(kc_ref, h, jnp.where(nope, rms64(slab) * gk, kr))
        put(vc_ref, h, jnp.where(ones_at_64, 1.0, vm[:, h * LANES:(h + 1) * LANES]))

    for j, slab in enumerate(_head_slabs(proj("d_q"))):
        put(qd_ref, j, nr64(slab, 4, 64 ** -0.5 * LOG2E))
    for j, slab in enumerate(_head_slabs(proj("d_k"))):
        put(kd_ref, j, nr64(slab, 5, 1.0))
    v = proj("d_v")
    ones_at_0 = jnp.where(lane == 0, 1.0, 0.0)
    for h in range(4):
        put(vd_ref, 2 * h, v[:, h * LANES:(h + 1) * LANES])
        put(vd_ref, 2 * h + 1, ones_at_0)


def _prep_call(xall, modsel, gain, wl, tabs, n_lat):
    r = xall.shape[0]
    nt = r // ROW_TILE
    row = lambda w: pl.BlockSpec((ROW_TILE, w), lambda i: (i, 0))
    out_w = dict(qa=1024, ka=256, va=256, qb=256, kb=256, vb=512, gb=512,
                 qc=1024, kc=1024, vc=1024, qd=1024, kd=1024, vd=1024)
    out_shape = [jax.ShapeDtypeStruct((r, w), F32 if n == "gb" else BF16) for n, w in out_w.items()]
    outs = pl.pallas_call(
        _prep_kernel,
        grid=(nt,),
        in_specs=[row(D_MODEL),
                  _mod_spec(n_lat // ROW_TILE),
                  _const_spec((1, D_MODEL)),
                  _const_spec((D_MODEL, MAIN_WIDTH)),
                  _const_spec((384, 1024)), _const_spec((256, 1024)), _const_spec((256, 1024)),
                  _const_spec((8, LANES)), _const_spec((1, 384)), _const_spec((1, 256)),
                  row(LANES), row(LANES), row(LANES), row(LANES)],
        out_specs=[row(w) for w in out_w.values()],
        out_shape=out_shape,
        compiler_params=_params(("arbitrary",)),
        name="prep",
    )(xall, modsel, gain, wl["w_main"], wl["wuq"], wl["wuk"], wl["wuv"],
      wl["head_gain"], wl["cq_gain"], wl["ckv_gain"], *tabs)
    return dict(zip(out_w.keys(), outs))


N_PAD_CHUNKS = CTX // RET_CHUNK


def _ret_kernel(dec_ref, qf_ref, kf_ref, vf_ref, qr_ref, kr_ref, vr_ref,
                of_ref, or_ref, state_ref, intra_ref, qdec_ref, kdec_ref, cdec_ref):
    c = RET_CHUNK
    t = pl.program_id(0)

    @pl.when(t == 0)
    def _tables():
        x = dec_ref[...]
        lg = jnp.minimum(x, 0.0) - jnp.log(1.0 + jnp.exp(-jnp.abs(x)))
        i_idx = lax.broadcasted_iota(jnp.int32, (c, c), 0).astype(F32)
        j_idx = lax.broadcasted_iota(jnp.int32, (c, c), 1).astype(F32)
        for d in range(2):
            for h in range(N_HEADS_RET):
                g = lg[d:d + 1, h:h + 1]
                if d == 0:
                    rel, qe, ke = i_idx - j_idx, i_idx + 1.0, c - 1.0 - i_idx
                else:
                    rel, qe, ke = j_idx - i_idx, c - i_idx, i_idx
                intra_ref[d, h] = jnp.where(rel >= 0, jnp.exp(g * jnp.maximum(rel, 0.0)), 0.0)
                qdec_ref[d, h] = jnp.exp(g * qe)
                kdec_ref[d, h] = jnp.exp(g * ke[:, :64])
                cdec_ref[d, h] = jnp.exp(jnp.broadcast_to(g, (8, LANES)) * float(c))

    @pl.when((t == 0) | (t == N_PAD_CHUNKS))
    def _clear():
        state_ref[...] = jnp.zeros_like(state_ref)

    for d, (q_ref, k_ref, v_ref, o_ref) in enumerate(
            ((qf_ref, kf_ref, vf_ref, of_ref), (qr_ref, kr_ref, vr_ref, or_ref))):
        for h in range(N_HEADS_RET):
            qh = q_ref[:, h * 64:(h + 1) * 64]
            kh = k_ref[:, h * 64:(h + 1) * 64]
            vh = v_ref[:, h * LANES:(h + 1) * LANES]
            sc = lax.dot_general(qh, kh, (((1,), (1,)), ((), ())), preferred_element_type=F32)
            sc = sc * intra_ref[d, h]
            st = state_ref[d, h]
            o = _dot(sc.astype(BF16), vh) + _dot(qh, st.astype(BF16)) * qdec_ref[d, h]
            o_ref[:, h * LANES:(h + 1) * LANES] = o
            kd = (kh.astype(F32) * kdec_ref[d, h]).astype(BF16)
            upd = lax.dot_general(kd, vh, (((0,), (0,)), ((), ())), preferred_element_type=F32)
            state_ref[d, h] = st * cdec_ref[d, h][0:1, :] + upd


def _ret_call(qb, kb, vb, dec, n_lat):
    r = qb.shape[0]
    nc = r // RET_CHUNK
    nl = n_lat // RET_CHUNK
    nctx = CTX // RET_CHUNK
    npad = N_PAD_CHUNKS
    assert nc == nl + nctx + npad
    fwd = lambda t: (jnp.where(t < npad, nl + nctx + t,
                               jnp.where(t < npad + nctx, nl + t - npad, t - npad - nctx)), 0)
    bwd = lambda t: (jnp.where(t < npad, nl + nctx + t, nc - 1 - t), 0)
    blk = lambda w, m: pl.BlockSpec((RET_CHUNK, w), m)
    return pl.pallas_call(
        _ret_kernel,
        grid=(nc,),
        in_specs=[_const_spec((8, LANES)),
                  blk(256, fwd), blk(256, fwd), blk(512, fwd),
                  blk(256, bwd), blk(256, bwd), blk(512, bwd)],
        out_specs=[blk(512, fwd), blk(512, bwd)],
        out_shape=[jax.ShapeDtypeStruct((r, 512), F32)] * 2,
        scratch_shapes=[pltpu.VMEM((2, N_HEADS_RET, 64, LANES), F32),
                        pltpu.VMEM((2, N_HEADS_RET, RET_CHUNK, RET_CHUNK), F32),
                        pltpu.VMEM((2, N_HEADS_RET, RET_CHUNK, RET_CHUNK), F32),
                        pltpu.VMEM((2, N_HEADS_RET, RET_CHUNK, 64), F32),
                        pltpu.VMEM((2, N_HEADS_RET, 8, LANES), F32)],
        compiler_params=_params(("arbitrary",)),
        name="retention",
    )(dec, qb, kb, vb, qb, kb, vb)


def _attn_kernel(*refs, nm, dv, n_lat, diff_scale):
    if diff_scale is None:
        qt_ref, k_ref, vl_ref, vc_ref, o_ref, m_ref, acc_ref, s_ref, p_ref = refs
    else:
        qt_ref, k_ref, vl_ref, vc_ref, lam_ref, sg_ref, o_ref, m_ref, acc_ref, s_ref, p_ref = refs
    i = pl.program_id(1)
    n_chunks = n_lat // KV_CHUNK
    maps = range(nm)

    def scores(k, a):
        return _dot(k[:, a * LANES:(a + 1) * LANES], qt_ref[a * LANES:(a + 1) * LANES, :])

    kc = k_ref[n_lat:n_lat + CTX, :]
    for a in maps:
        s = scores(kc, a)
        m = jnp.max(s, axis=0, keepdims=True)
        m_ref[a] = m
        acc_ref[a] = _dot(vc_ref[0], jnp.exp2(s - m).astype(BF16))

    def qk(j, slot):
        k = k_ref[pl.ds(pl.multiple_of(j * KV_CHUNK, KV_CHUNK), KV_CHUNK), :]
        for a in maps:
            s_ref[a, slot] = scores(k, a)

    def pv(j, slot):
        vt = vl_ref[0, j]
        return [_dot(vt, p_ref[a, slot]) for a in maps]

    def softmax(slot):
        alphas = []
        for a in maps:
            s = s_ref[a, slot]
            m_old = m_ref[a]
            m_new = jnp.maximum(m_old, jnp.max(s, axis=0, keepdims=True))
            p_ref[a, slot] = jnp.exp2(s - m_new).astype(BF16)
            m_ref[a] = m_new
            alphas.append(jnp.exp2(m_old - m_new))
        return tuple(alphas)

    def accumulate(alphas, pvs):
        for a in maps:
            acc_ref[a] = acc_ref[a] * alphas[a] + pvs[a]

    @pl.when(i < n_lat // ATT_TQ)
    def _latent_keys():
        qk(0, 0)
        p_ref[:, 1] = jnp.zeros((nm,) + p_ref.shape[2:], BF16)
        ones = tuple(jnp.ones((1, ATT_TQ), F32) for _ in maps)

        def stage(j, slot, alphas, prefetch=True):
            pvs = pv(jnp.maximum(j - 1, 0), 1 - slot)
            if prefetch:
                qk(j + 1, 1 - slot)
            new_alphas = softmax(slot)
            accumulate(alphas, pvs)
            return new_alphas

        def pair(t, alphas):
            return stage(2 * t + 1, 1, stage(2 * t, 0, alphas))

        alphas = lax.fori_loop(0, n_chunks // 2 - 1, pair, ones)
        alphas = stage(n_chunks - 2, 0, alphas)
        alphas = stage(n_chunks - 1, 1, alphas, prefetch=False)
        accumulate(alphas, pv(n_chunks - 1, 1))

    def normalised(a):
        acc = acc_ref[a]
        return acc[:dv] * (1.0 / acc[dv:dv + 1])

    if diff_scale is None:
        o = normalised(0)
        o = jnp.concatenate([o, jnp.zeros((LANES - dv, o.shape[1]), F32)], axis=0)
        o_ref[...] = o.T.astype(o_ref.dtype)
    else:
        lv = lam_ref[...]
        lam_init = 1.0 - diff_scale
        lam = (jnp.exp(jnp.sum(lv[0:1] * lv[1:2], axis=-1, keepdims=True))
               - jnp.exp(jnp.sum(lv[2:3] * lv[3:4], axis=-1, keepdims=True)) + lam_init)
        o = (normalised(0) - lam * normalised(1)).T
        o_ref[...] = (_rms(o, float(dv)) * sg_ref[...] * diff_scale).astype(o_ref.dtype)


def _attn_call(q, k, v, *, n_heads, n_kv, nm, dv, n_lat, name, extra=(), diff_scale=None):
    r = q.shape[0]
    nq = r // ATT_TQ
    group = n_heads // n_kv
    dvp = -(-(dv + 1) // BF16_ROWS) * BF16_ROWS
    slab = v.shape[1] // n_kv
    n_chunks = n_lat // KV_CHUNK
    qt = q.T
    v3 = v.reshape(r, n_kv, slab)[:, :, :dvp]
    vl = v3[:n_lat].reshape(n_chunks, KV_CHUNK, n_kv, dvp).transpose(2, 0, 3, 1)
    vc = v3[n_lat:n_lat + CTX].transpose(1, 2, 0)
    kern = functools.partial(_attn_kernel, nm=nm, dv=dv, n_lat=n_lat, diff_scale=diff_scale)
    extra_specs = [pl.BlockSpec(e.shape, lambda h, i: (0, 0)) for e in extra]
    return pl.pallas_call(
        kern,
        grid=(n_heads, nq),
        in_specs=[pl.BlockSpec((nm * LANES, ATT_TQ), lambda h, i: (h, i)),
                  pl.BlockSpec((r, nm * LANES), lambda h, i: (0, h // group)),
                  pl.BlockSpec((1, n_chunks, dvp, KV_CHUNK), lambda h, i: (h // group, 0, 0, 0)),
                  pl.BlockSpec((1, dvp, CTX), lambda h, i: (h // group, 0, 0))] + extra_specs,
        out_specs=pl.BlockSpec((ATT_TQ, LANES), lambda h, i: (i, h)),
        out_shape=jax.ShapeDtypeStruct((r, n_heads * LANES), BF16),
        scratch_shapes=[pltpu.VMEM((nm, 1, ATT_TQ), F32),
                        pltpu.VMEM((nm, dvp, ATT_TQ), F32),
                        pltpu.VMEM((nm, 2, KV_CHUNK, ATT_TQ), F32),
                        pltpu.VMEM((nm, 2, KV_CHUNK, ATT_TQ), BF16)],
        compiler_params=_params(("arbitrary", "arbitrary")),
        name=name,
    )(qt, k, vl, vc, *extra)


def _merge_kernel(x_ref, mod_ref, gain_ref, wg_ref, oa_ref, of_ref, or_ref, gb_ref, oc_ref, od_ref,
                  wa_ref, wb_ref, wc_ref, wd_ref, wo_ref, out_ref):
    x = x_ref[...]
    mod = mod_ref[0]
    hb = _prenorm(x, gain_ref[...], mod, 0).astype(BF16)
    ob = of_ref[...] + or_ref[...]
    g = gb_ref[...]
    parts = [_rms(ob[:, h * LANES:(h + 1) * LANES], float(LANES)) for h in range(N_HEADS_RET)]
    bb = (jnp.concatenate(parts, axis=-1) * _silu(g)).astype(BF16)
    branches = ((oa_ref[...], wa_ref), (bb, wb_ref), (oc_ref[...], wc_ref), (od_ref[...], wd_ref))
    acc = None
    for n, (b, w_ref) in enumerate(branches):
        gate = _sigmoid(_dot(hb, wg_ref[:, n * D_MODEL:(n + 1) * D_MODEL]))
        term = gate * _dot(b, w_ref[...])
        acc = term if acc is None else acc + term
    y = _dot(acc.astype(BF16), wo_ref[...])
    out_ref[...] = x + mod[2:3, :] * y


def _merge_call(xall, modsel, gain, wl, oa, of, orv, gb, oc, od, *, n_rows, n_lat):
    nt = n_rows // ROW_TILE
    row = lambda w: pl.BlockSpec((ROW_TILE, w), lambda i: (i, 0))
    return pl.pallas_call(
        _merge_kernel,
        grid=(nt,),
        in_specs=[row(D_MODEL),
                  _mod_spec(n_lat // ROW_TILE),
                  _const_spec((1, D_MODEL)),
                  _const_spec((D_MODEL, 4 * D_MODEL)),
                  row(1024), row(512), row(512), row(512), row(1024), row(512),
                  _const_spec((1024, D_MODEL)), _const_spec((512, D_MODEL)),
                  _const_spec((1024, D_MODEL)), _const_spec((512, D_MODEL)),
                  _const_spec((D_MODEL, D_MODEL))],
        out_specs=row(D_MODEL),
        out_shape=jax.ShapeDtypeStruct((n_rows, D_MODEL), F32),
        compiler_params=_params(("arbitrary",)),
        name="merge",
    )(xall, modsel, gain, wl["w_gate"], oa, of, orv, gb, oc, od,
      wl["wb_a"], wl["wb_b"], wl["wb_c"], wl["wb_d"], wl["w_out"])


def _ffn_kernel(x_ref, mod_ref, gain_ref, wi_ref, wo_ref, out_ref):
    x = x_ref[...]
    mod = mod_ref[0]
    hb = _prenorm(x, gain_ref[...], mod, 3).astype(BF16)
    gate = _dot(hb, wi_ref[:, :FFN_HIDDEN])
    up = _dot(hb, wi_ref[:, FFN_HIDDEN:])
    y = _dot((_silu(gate) * up).astype(BF16), wo_ref[...])
    out_ref[...] = x + mod[5:6, :] * y


def _ffn_call(x1, modsel, gain, wl, n_lat):
    r = x1.shape[0]
    nt = r // ROW_TILE
    row = pl.BlockSpec((ROW_TILE, D_MODEL), lambda i: (i, 0))
    return pl.pallas_call(
        _ffn_kernel,
        grid=(nt,),
        in_specs=[row, _mod_spec(n_lat // ROW_TILE),
                  _const_spec((1, D_MODEL)),
                  _const_spec((D_MODEL, 2 * FFN_HIDDEN)),
                  _const_spec((FFN_HIDDEN, D_MODEL))],
        out_specs=row,
        out_shape=jax.ShapeDtypeStruct((r, D_MODEL), F32),
        compiler_params=_params(("arbitrary",)),
        name="ffn",
    )(x1, modsel, gain, wl["w_ffn_in"], wl["w_ffn_out"])


def _pad_rows_per_head(w, n_heads, width):
    w = w.reshape(n_heads, width, w.shape[-1])
    w = jnp.pad(w, ((0, 0), (0, LANES - width), (0, 0)))
    return w.reshape(n_heads * LANES, -1)


def _pad_cols_per_head(w, n_heads, width):
    w = w.reshape(w.shape[0], n_heads, width)
    w = jnp.pad(w, ((0, 0), (0, 0), (0, LANES - width)))
    return w.reshape(w.shape[0], n_heads * LANES)


def _pad_vec(v):
    return jnp.pad(v, (0, LANES - v.shape[0]))


def _layer_weights(l, w_in, gqa_qk_gain, mla_cq_gain, mla_ckv_gain, mla_w_uq, mla_w_ukv,
                   mla_qk_gain, diff_qk_gain, w_branch, w_out, w_ffn_in, w_ffn_out):
    wi = w_in[l]
    w_main = jnp.concatenate([wi[:, :2976], jnp.zeros((D_MODEL, 96), F32), wi[:, 2976:4512]], axis=1)
    ukv = mla_w_ukv[l].reshape(256, 8, 128)
    head_gain = jnp.stack([
        _pad_vec(gqa_qk_gain[l, 0]), _pad_vec(gqa_qk_gain[l, 1]),
        _pad_vec(mla_qk_gain[l, 0]), _pad_vec(mla_qk_gain[l, 1]),
        _pad_vec(diff_qk_gain[l, 0]), _pad_vec(diff_qk_gain[l, 1]),
        jnp.zeros((LANES,), F32), jnp.zeros((LANES,), F32)])
    wb = w_branch[l]
    return dict(
        w_main=w_main.astype(BF16),
        w_gate=wi[:, 4512:].astype(BF16),
        wuq=_pad_cols_per_head(mla_w_uq[l], 8, 96).astype(BF16),
        wuk=_pad_cols_per_head(ukv[:, :, :64].reshape(256, 512), 8, 64).astype(BF16),
        wuv=_pad_cols_per_head(ukv[:, :, 64:].reshape(256, 512), 8, 64).astype(BF16),
        head_gain=head_gain,
        cq_gain=mla_cq_gain[l][None], ckv_gain=mla_ckv_gain[l][None],
        wb_a=_pad_rows_per_head(wb[0], 8, 64).astype(BF16),
        wb_b=wb[1].astype(BF16),
        wb_c=_pad_rows_per_head(wb[2], 8, 64).astype(BF16),
        wb_d=wb[3].astype(BF16),
        w_out=w_out[l].astype(BF16),
        w_ffn_in=w_ffn_in[l].astype(BF16),
        w_ffn_out=w_ffn_out[l].astype(BF16),
    )


def _rope_tables(seq, n_rest):
    n_rows = seq // GRID_W
    row = jnp.repeat(jnp.arange(n_rows, dtype=F32), GRID_W)
    col = jnp.tile(jnp.arange(GRID_W, dtype=F32), n_rows)

    def pairs(dim):
        quarter = dim // 4
        freqs = ROPE_THETA ** (-jnp.arange(quarter, dtype=F32) / quarter)
        ang = jnp.concatenate([row[:, None] * freqs, col[:, None] * freqs], axis=-1)
        cos = jnp.repeat(jnp.cos(ang), 2, axis=-1)
        sin = jnp.repeat(jnp.sin(ang), 2, axis=-1) * jnp.tile(jnp.array([-1.0, 1.0], F32), dim // 2)
        return cos, sin

    c64, s64 = pairs(64)
    c32, s32 = pairs(32)
    one = lambda w: jnp.ones((seq, w), F32)
    zero = lambda w: jnp.zeros((seq, w), F32)
    t64c = jnp.concatenate([c64, c64], axis=-1)
    t64s = jnp.concatenate([s64, s64], axis=-1)
    tmc = jnp.concatenate([one(64), c32, one(32)], axis=-1)
    tms = jnp.concatenate([zero(64), s32, zero(32)], axis=-1)
    rest_c = jnp.ones((n_rest, LANES), F32)
    rest_s = jnp.zeros((n_rest, LANES), F32)
    cat = lambda t, rest: jnp.concatenate([t, rest], axis=0)
    return cat(t64c, rest_c), cat(t64s, rest_s), cat(tmc, rest_c), cat(tms, rest_s)


def kernel(x, c, ctx, c_ctx, w_mod, b_mod, norm_gain, w_in, gqa_qk_gain, ret_decay, mla_cq_gain,
           mla_ckv_gain, mla_w_uq, mla_w_ukv, mla_qk_gain, diff_qk_gain, diff_lambda, diff_subln_gain,
           w_branch, w_out, w_ffn_in, w_ffn_out):
    depth = w_mod.shape[0]
    batch, seq, _ = x.shape
    assert batch == 1 and ctx.shape[1] == CTX and seq % KV_CHUNK == 0 and seq % ATT_TQ == 0
    n_all = seq + 2 * CTX

    tabs = _rope_tables(seq, 2 * CTX)
    cc = jnp.concatenate([c_ctx[None], c, jnp.zeros((6, D_MODEL), F32)], axis=0)
    mods = _mod_call(cc, w_mod, b_mod)
    xall = jnp.concatenate([x[0], ctx[0], jnp.zeros((CTX, D_MODEL), F32)], axis=0)

    for l in range(depth):
        last = l == depth - 1
        lam_init = 0.8 - 0.6 * math.exp(-0.3 * l)
        wl = _layer_weights(l, w_in, gqa_qk_gain, mla_cq_gain, mla_ckv_gain, mla_w_uq, mla_w_ukv,
                            mla_qk_gain, diff_qk_gain, w_branch, w_out, w_ffn_in, w_ffn_out)
        modsel = jnp.pad(mods[l, :2].reshape(2, 6, D_MODEL), ((0, 0), (0, 2), (0, 0)))
        g_attn = norm_gain[l, 0][None]
        g_ffn = norm_gain[l, 1][None]

        p = _prep_call(xall, modsel, g_attn, wl, tabs, seq)
        dec = jnp.pad(ret_decay[l].astype(F32), ((0, 6), (0, LANES - N_HEADS_RET)))
        of, orv = _ret_call(p["qb"], p["kb"], p["vb"], dec, seq)
        oa = _attn_call(p["qa"], p["ka"], p["va"], n_heads=8, n_kv=2, nm=1, dv=64, n_lat=seq,
                        name="attn_gqa")
        oc = _attn_call(p["qc"], p["kc"], p["vc"], n_heads=8, n_kv=8, nm=1, dv=64, n_lat=seq,
                        name="attn_mla")
        lam_tab = jnp.pad(diff_lambda[l].astype(F32), ((0, 4), (0, LANES - 64)))
        od = _attn_call(p["qd"], p["kd"], p["vd"], n_heads=4, n_kv=4, nm=2, dv=128, n_lat=seq,
                        name="attn_diff", extra=(lam_tab, diff_subln_gain[l][None]),
                        diff_scale=1.0 - lam_init)
        n_rows = seq if last else n_all
        x1 = _merge_call(xall, modsel, g_attn, wl, oa, of, orv, p["gb"], oc, od, n_rows=n_rows, n_lat=seq)
        xall = _ffn_call(x1, modsel, g_ffn, wl, seq)

    return xall[None]
```

```python
import functools
import math

import jax
import jax.numpy as jnp
from jax import lax
from jax.experimental import pallas as pl
from jax.experimental.pallas import tpu as pltpu

F32 = jnp.float32
BF16 = jnp.bfloat16

D_MODEL = 1024
GRID_W = 64
ROPE_THETA = 10000.0
EPS = 1e-6
RET_CHUNK = 128
N_HEADS_RET = 4
FFN_HIDDEN = 2816
LANES = 128
BF16_ROWS = 16
ROW_TILE = 256
CTX = 256
PAD = 768
ATT_TQ = 1024
Q_BLOCK = 256
KV_CHUNK = 512
STAGES_PER_TRIP = 2
LOG2E = 1.4426950408889634
VMEM_LIMIT = 56 * 1024 * 1024

SEG = dict(a_q=(0, 512), a_k=(512, 640), a_v=(640, 768),
           b_q=(768, 1024), b_k=(1024, 1280), b_v=(1280, 1792), b_g=(1792, 2304),
           c_q=(2304, 2688), c_kv=(2688, 2944), c_kr=(2944, 3072),
           d_q=(3072, 3584), d_k=(3584, 4096), d_v=(4096, 4608))
MAIN_WIDTH = 4608


def _params(sem):
    return pltpu.CompilerParams(dimension_semantics=sem, vmem_limit_bytes=VMEM_LIMIT)


def _const_spec(shape):
    n = len(shape)
    return pl.BlockSpec(shape, lambda *_: (0,) * n, pipeline_mode=pl.Buffered(1))


def _mod_spec(n_lat_tiles):
    return pl.BlockSpec((1, 8, D_MODEL), lambda i: (jnp.where(i < n_lat_tiles, 1, 0), 0, 0))


def _sigmoid(x):
    return 1.0 / (1.0 + jnp.exp(-x))


def _silu(x):
    return x * _sigmoid(x)


def _rms(x, n):
    ss = jnp.sum(x * x, axis=-1, keepdims=True) * (1.0 / n)
    return x * lax.rsqrt(ss + EPS)


def _lane(shape):
    return lax.broadcasted_iota(jnp.int32, shape, 1)


def _rope(y, cos, sin_signed):
    even = (_lane(y.shape) & 1) == 0
    partner = jnp.where(even, pltpu.roll(y, LANES - 1, 1), pltpu.roll(y, 1, 1))
    return y * cos + partner * sin_signed


def _head_slabs(p):
    lo = _lane((p.shape[0], LANES)) < 64
    out = []
    for j in range(p.shape[1] // LANES):
        src = p[:, j * LANES:(j + 1) * LANES]
        out.append(jnp.where(lo, src, 0.0))
        out.append(jnp.where(lo, pltpu.roll(src, 64, 1), 0.0))
    return out


def _dot(a, b):
    return jnp.dot(a, b, preferred_element_type=F32)


def _mod_kernel(c_ref, w_ref, b_ref, o_ref):
    s = _silu(c_ref[...]).astype(BF16)
    o_ref[0] = _dot(s, w_ref[0].astype(BF16)) + b_ref[0]


def _mod_call(cc, w_mod, b_mod):
    depth = w_mod.shape[0]
    nt = 6
    return pl.pallas_call(
        _mod_kernel,
        grid=(depth, nt),
        in_specs=[pl.BlockSpec((8, D_MODEL), lambda l, j: (0, 0)),
                  pl.BlockSpec((1, D_MODEL, D_MODEL), lambda l, j: (l, 0, j)),
                  pl.BlockSpec((1, 1, D_MODEL), lambda l, j: (l, 0, j))],
        out_specs=pl.BlockSpec((1, 8, D_MODEL), lambda l, j: (l, 0, j)),
        out_shape=jax.ShapeDtypeStruct((depth, 8, 6 * D_MODEL), F32),
        compiler_params=_params(("arbitrary", "arbitrary")),
        name="mod",
    )(cc, w_mod, b_mod.reshape(depth, 1, 6 * D_MODEL))


def _prenorm(x, gain, mod, shift_row):
    h = _rms(x, D_MODEL) * gain
    return h * (1.0 + mod[shift_row + 1:shift_row + 2, :]) + mod[shift_row:shift_row + 1, :]


def _prep_kernel(x_ref, mod_ref, gain_ref, w_ref, wuq_ref, wuk_ref, wuv_ref,
                 hg_ref, cqg_ref, ckvg_ref, c64_ref, s64_ref, cm_ref, sm_ref,
                 qa_ref, ka_ref, va_ref, qb_ref, kb_ref, vb_ref, gb_ref,
                 qc_ref, kc_ref, vc_ref, qd_ref, kd_ref, vd_ref):
    hb = _prenorm(x_ref[...], gain_ref[...], mod_ref[0], 0).astype(BF16)
    t = hb.shape[0]
    lane = _lane((t, LANES))
    c64, s64 = c64_ref[...], s64_ref[...]
    cm, sm = cm_ref[...], sm_ref[...]

    def proj(name):
        a, b = SEG[name]
        return _dot(hb, w_ref[:, a:b])

    def put(ref, j, val):
        ref[:, j * LANES:(j + 1) * LANES] = val.astype(ref.dtype)

    def rms64(slab):
        ss = jnp.sum(slab * slab, axis=-1, keepdims=True) * (1.0 / 64.0)
        return slab * lax.rsqrt(ss + EPS)

    def nr64(slab, gain_row, scale):
        y = rms64(slab) * hg_ref[gain_row:gain_row + 1, :]
        y = _rope(y, c64, s64)
        return y * scale if scale != 1.0 else y

    ones_at_64 = lane == 64

    for j, slab in enumerate(_head_slabs(proj("a_q"))):
        put(qa_ref, j, nr64(slab, 0, 64 ** -0.5 * LOG2E))
    for j, slab in enumerate(_head_slabs(proj("a_k"))):
        put(ka_ref, j, nr64(slab, 1, 1.0))
    for j, slab in enumerate(_head_slabs(proj("a_v"))):
        put(va_ref, j, jnp.where(ones_at_64, 1.0, slab))

    q = proj("b_q")
    k = proj("b_k") * (64 ** -0.5)
    for j in range(2):
        put(qb_ref, j, _rope(q[:, j * LANES:(j + 1) * LANES], c64, s64))
        put(kb_ref, j, _rope(k[:, j * LANES:(j + 1) * LANES], c64, s64))
    vb_ref[...] = proj("b_v").astype(vb_ref.dtype)
    gb_ref[...] = proj("b_g")

    nope = lane < 64
    ropel = (lane >= 64) & (lane < 96)
    cq = (_rms(proj("c_q"), 384.0) * cqg_ref[...]).astype(BF16)
    qm = _dot(cq, wuq_ref[...])
    gq = hg_ref[2:3, :]
    for h in range(8):
        slab = qm[:, h * LANES:(h + 1) * LANES]
        sq = slab * slab
        ssn = jnp.sum(jnp.where(nope, sq, 0.0), axis=-1, keepdims=True) * (1.0 / 64.0)
        ssr = jnp.sum(jnp.where(ropel, sq, 0.0), axis=-1, keepdims=True) * (1.0 / 32.0)
        inv = jnp.where(nope, lax.rsqrt(ssn + EPS), lax.rsqrt(ssr + EPS))
        y = _rope(slab * inv * gq, cm, sm) * (96 ** -0.5 * LOG2E)
        put(qc_ref, h, y)
    ckv = (_rms(proj("c_kv"), 256.0) * ckvg_ref[...]).astype(BF16)
    kn = _dot(ckv, wuk_ref[...])
    vm = _dot(ckv, wuv_ref[...])
    gk = hg_ref[3:4, :]
    kr = pltpu.roll(proj("c_kr"), 64, 1)
    ssk = jnp.sum(kr * kr, axis=-1, keepdims=True) * (1.0 / 32.0)
    kr = _rope(kr * lax.rsqrt(ssk + EPS) * gk, cm, sm)
    kr = jnp.where(ropel, kr, 0.0)
    for h in range(8):
        slab = kn[:, h * LANES:(h + 1) * LANES]
        put(kc_ref, h, jnp.where(nope, rms64(slab) * gk, kr))
        put(vc_ref, h, jnp.where(ones_at_64, 1.0, vm[:, h * LANES:(h + 1) * LANES]))

    for j, slab in enumerate(_head_slabs(proj("d_q"))):
        put(qd_ref, j, nr64(slab, 4, 64 ** -0.5 * LOG2E))
    for j, slab in enumerate(_head_slabs(proj("d_k"))):
        put(kd_ref, j, nr64(slab, 5, 1.0))
    v = proj("d_v")
    ones_at_0 = jnp.where(lane == 0, 1.0, 0.0)
    for h in range(4):
        put(vd_ref, 2 * h, v[:, h * LANES:(h + 1) * LANES])
        put(vd_ref, 2 * h + 1, ones_at_0)


def _prep_call(xall, modsel, gain, wl, tabs, n_lat):
    r = xall.shape[0]
    nt = r // ROW_TILE
    row = lambda w: pl.BlockSpec((ROW_TILE, w), lambda i: (i, 0))
    out_w = dict(qa=1024, ka=256, va=256, qb=256, kb=256, vb=512, gb=512,
                 qc=1024, kc=1024, vc=1024, qd=1024, kd=1024, vd=1024)
    out_shape = [jax.ShapeDtypeStruct((r, w), F32 if n == "gb" else BF16) for n, w in out_w.items()]
    outs = pl.pallas_call(
        _prep_kernel,
        grid=(nt,),
        in_specs=[row(D_MODEL),
                  _mod_spec(n_lat // ROW_TILE),
                  _const_spec((1, D_MODEL)),
                  _const_spec((D_MODEL, MAIN_WIDTH)),
                  _const_spec((384, 1024)), _const_spec((256, 1024)), _const_spec((256, 1024)),
                  _const_spec((8, LANES)), _const_spec((1, 384)), _const_spec((1, 256)),
                  row(LANES), row(LANES), row(LANES), row(LANES)],
        out_specs=[row(w) for w in out_w.values()],
        out_shape=out_shape,
        compiler_params=_params(("arbitrary",)),
        name="prep",
    )(xall, modsel, gain, wl["w_main"], wl["wuq"], wl["wuk"], wl["wuv"],
      wl["head_gain"], wl["cq_gain"], wl["ckv_gain"], *tabs)
    return dict(zip(out_w.keys(), outs))


N_PAD_CHUNKS = PAD // RET_CHUNK


def _ret_kernel(dec_ref, qf_ref, kf_ref, vf_ref, qr_ref, kr_ref, vr_ref,
                of_ref, or_ref, state_ref, intra_ref, qdec_ref, kdec_ref, cdec_ref):
    c = RET_CHUNK
    t = pl.program_id(0)

    @pl.when(t == 0)
    def _tables():
        x = dec_ref[...]
        lg = jnp.minimum(x, 0.0) - jnp.log(1.0 + jnp.exp(-jnp.abs(x)))
        i_idx = lax.broadcasted_iota(jnp.int32, (c, c), 0).astype(F32)
        j_idx = lax.broadcasted_iota(jnp.int32, (c, c), 1).astype(F32)
        for d in range(2):
            for h in range(N_HEADS_RET):
                g = lg[d:d + 1, h:h + 1]
                if d == 0:
                    rel, qe, ke = i_idx - j_idx, i_idx + 1.0, c - 1.0 - i_idx
                else:
                    rel, qe, ke = j_idx - i_idx, c - i_idx, i_idx
                intra_ref[d, h] = jnp.where(rel >= 0, jnp.exp(g * jnp.maximum(rel, 0.0)), 0.0)
                qdec_ref[d, h] = jnp.exp(g * qe)
                kdec_ref[d, h] = jnp.exp(g * ke[:, :64])
                cdec_ref[d, h] = jnp.exp(jnp.broadcast_to(g, (8, LANES)) * float(c))

    @pl.when((t == 0) | (t == N_PAD_CHUNKS))
    def _clear():
        state_ref[...] = jnp.zeros_like(state_ref)

    for d, (q_ref, k_ref, v_ref, o_ref) in enumerate(
            ((qf_ref, kf_ref, vf_ref, of_ref), (qr_ref, kr_ref, vr_ref, or_ref))):
        for h in range(N_HEADS_RET):
            qh = q_ref[:, h * 64:(h + 1) * 64]
            kh = k_ref[:, h * 64:(h + 1) * 64]
            vh = v_ref[:, h * LANES:(h + 1) * LANES]
            sc = lax.dot_general(qh, kh, (((1,), (1,)), ((), ())), preferred_element_type=F32)
            sc = sc * intra_ref[d, h]
            st = state_ref[d, h]
            o = _dot(sc.astype(BF16), vh) + _dot(qh, st.astype(BF16)) * qdec_ref[d, h]
            o_ref[:, h * LANES:(h + 1) * LANES] = o
            kd = (kh.astype(F32) * kdec_ref[d, h]).astype(BF16)
            upd = lax.dot_general(kd, vh, (((0,), (0,)), ((), ())), preferred_element_type=F32)
            state_ref[d, h] = st * cdec_ref[d, h][0:1, :] + upd


def _ret_call(qb, kb, vb, dec, n_lat):
    r = qb.shape[0]
    nc = r // RET_CHUNK
    nl = n_lat // RET_CHUNK
    nctx = CTX // RET_CHUNK
    npad = N_PAD_CHUNKS
    assert nc == nl + nctx + npad
    fwd = lambda t: (jnp.where(t < npad, nl + nctx + t,
                               jnp.where(t < npad + nctx, nl + t - npad, t - npad - nctx)), 0)
    bwd = lambda t: (jnp.where(t < npad, nl + nctx + t, nc - 1 - t), 0)
    blk = lambda w, m: pl.BlockSpec((RET_CHUNK, w), m)
    return pl.pallas_call(
        _ret_kernel,
        grid=(nc,),
        in_specs=[_const_spec((8, LANES)),
                  blk(256, fwd), blk(256, fwd), blk(512, fwd),
                  blk(256, bwd), blk(256, bwd), blk(512, bwd)],
        out_specs=[blk(512, fwd), blk(512, bwd)],
        out_shape=[jax.ShapeDtypeStruct((r, 512), F32)] * 2,
        scratch_shapes=[pltpu.VMEM((2, N_HEADS_RET, 64, LANES), F32),
                        pltpu.VMEM((2, N_HEADS_RET, RET_CHUNK, RET_CHUNK), F32),
                        pltpu.VMEM((2, N_HEADS_RET, RET_CHUNK, RET_CHUNK), F32),
                        pltpu.VMEM((2, N_HEADS_RET, RET_CHUNK, 64), F32),
                        pltpu.VMEM((2, N_HEADS_RET, 8, LANES), F32)],
        compiler_params=_params(("arbitrary",)),
        name="retention",
    )(dec, qb, kb, vb, qb, kb, vb)


def _attn_kernel(*refs, nm, dv, n_lat, diff_scale):
    if diff_scale is None:
        qt_ref, k_ref, vl_ref, vc_ref, o_ref, m_ref, acc_ref, s_ref, p_ref, mx_ref, alpha_ref = refs
    else:
        qt_ref, k_ref, vl_ref, vc_ref, lam_ref, sg_ref, o_ref, m_ref, acc_ref, s_ref, p_ref, mx_ref, alpha_ref = refs
    i = pl.program_id(1)
    n_chunks = n_lat // KV_CHUNK
    maps = range(nm)

    def scores(k, a):
        return _dot(k[:, a * LANES:(a + 1) * LANES], qt_ref[a * LANES:(a + 1) * LANES, :])

    kc = k_ref[n_lat:n_lat + CTX, :]
    for a in maps:
        s = scores(kc, a)
        m = jnp.max(s, axis=0, keepdims=True)
        m_ref[a] = m
        acc_ref[a] = _dot(vc_ref[0], jnp.exp2(s - m).astype(BF16))

    def qk(j, slot):
        k = k_ref[pl.ds(pl.multiple_of(j * KV_CHUNK, KV_CHUNK), KV_CHUNK), :]
        for a in maps:
            s = scores(k, a)
            s_ref[a, slot] = s
            mx_ref[a, slot] = jnp.max(s, axis=0, keepdims=True)

    @pl.when(i < n_lat // ATT_TQ)
    def _latent_keys():
        qk(0, 0)
        p_ref[:, 1] = jnp.zeros((nm,) + p_ref.shape[2:], BF16)
        alpha_ref[...] = jnp.ones_like(alpha_ref)

        def stage(j, slot, prefetch=True, softmax=True):
            other = 1 - slot
            if isinstance(j, int):
                j_prev, k_start = max(j - 1, 0), (j + 1) * KV_CHUNK
            else:
                j_prev, k_start = jnp.maximum(j - 1, 0), pl.multiple_of((j + 1) * KV_CHUNK, KV_CHUNK)
            k_rows = pl.ds(k_start, KV_CHUNK)
            for c in range(ATT_TQ // Q_BLOCK):
                cols = slice(c * Q_BLOCK, (c + 1) * Q_BLOCK)
                for a in maps:
                    heads = slice(a * LANES, (a + 1) * LANES)
                    if softmax:
                        m_old = m_ref[a, :, cols]
                        m_new = jnp.maximum(m_old, mx_ref[a, slot, :, cols])
                        p_ref[a, slot, :, cols] = jnp.exp2(s_ref[a, slot, :, cols] - m_new).astype(BF16)
                        m_ref[a, :, cols] = m_new
                    pv_c = _dot(vl_ref[0, j_prev], p_ref[a, other, :, cols])
                    if prefetch:
                        s = _dot(k_ref[k_rows, heads], qt_ref[heads, cols])
                        s_ref[a, other, :, cols] = s
                        mx_ref[a, other, :, cols] = jnp.max(s, axis=0, keepdims=True)
                    acc_ref[a, :, cols] = acc_ref[a, :, cols] * alpha_ref[a, :, cols] + pv_c
                    if softmax:
                        alpha_ref[a, :, cols] = jnp.exp2(m_old - m_new)

        def trip(t, carry):
            for u in range(STAGES_PER_TRIP):
                stage(STAGES_PER_TRIP * t + u, u % 2)
            return carry

        n_trips = n_chunks // STAGES_PER_TRIP - 1
        lax.fori_loop(0, n_trips, trip, 0)
        for j in range(n_trips * STAGES_PER_TRIP, n_chunks):
            stage(j, j % 2, prefetch=j + 1 < n_chunks)
        stage(n_chunks, n_chunks % 2, prefetch=False, softmax=False)

    def normalised(a):
        acc = acc_ref[a]
        return acc[:dv] * (1.0 / acc[dv:dv + 1])

    if diff_scale is None:
        o = normalised(0)
        o = jnp.concatenate([o, jnp.zeros((LANES - dv, o.shape[1]), F32)], axis=0)
        o_ref[...] = o.T.astype(o_ref.dtype)
    else:
        lv = lam_ref[...]
        lam_init = 1.0 - diff_scale
        lam = (jnp.exp(jnp.sum(lv[0:1] * lv[1:2], axis=-1, keepdims=True))
               - jnp.exp(jnp.sum(lv[2:3] * lv[3:4], axis=-1, keepdims=True)) + lam_init)
        o = (normalised(0) - lam * normalised(1)).T
        o_ref[...] = (_rms(o, float(dv)) * sg_ref[...] * diff_scale).astype(o_ref.dtype)


def _attn_call(q, k, v, *, n_heads, n_kv, nm, dv, n_lat, name, extra=(), diff_scale=None):
    r = q.shape[0]
    nq = r // ATT_TQ
    group = n_heads // n_kv
    dvp = -(-(dv + 1) // BF16_ROWS) * BF16_ROWS
    slab = v.shape[1] // n_kv
    n_chunks = n_lat // KV_CHUNK
    qt = q.T
    v3 = v.reshape(r, n_kv, slab)[:, :, :dvp]
    vl = v3[:n_lat].reshape(n_chunks, KV_CHUNK, n_kv, dvp).transpose(2, 0, 3, 1)
    vc = v3[n_lat:n_lat + CTX].transpose(1, 2, 0)
    kern = functools.partial(_attn_kernel, nm=nm, dv=dv, n_lat=n_lat, diff_scale=diff_scale)
    extra_specs = [pl.BlockSpec(e.shape, lambda h, i: (0, 0)) for e in extra]
    return pl.pallas_call(
        kern,
        grid=(n_heads, nq),
        in_specs=[pl.BlockSpec((nm * LANES, ATT_TQ), lambda h, i: (h, i)),
                  pl.BlockSpec((r, nm * LANES), lambda h, i: (0, h // group)),
                  pl.BlockSpec((1, n_chunks, dvp, KV_CHUNK), lambda h, i: (h // group, 0, 0, 0)),
                  pl.BlockSpec((1, dvp, CTX), lambda h, i: (h // group, 0, 0))] + extra_specs,
        out_specs=pl.BlockSpec((ATT_TQ, LANES), lambda h, i: (i, h)),
        out_shape=jax.ShapeDtypeStruct((r, n_heads * LANES), BF16),
        scratch_shapes=[pltpu.VMEM((nm, 1, ATT_TQ), F32),
                        pltpu.VMEM((nm, dvp, ATT_TQ), F32),
                        pltpu.VMEM((nm, 2, KV_CHUNK, ATT_TQ), F32),
                        pltpu.VMEM((nm, 2, KV_CHUNK, ATT_TQ), BF16),
                        pltpu.VMEM((nm, 2, 1, ATT_TQ), F32),
                        pltpu.VMEM((nm, 1, ATT_TQ), F32)],
        compiler_params=_params(("arbitrary", "arbitrary")),
        name=name,
    )(qt, k, vl, vc, *extra)


def _merge_kernel(x_ref, mod_ref, gain_ref, wg_ref, oa_ref, of_ref, or_ref, gb_ref, oc_ref, od_ref,
                  wa_ref, wb_ref, wc_ref, wd_ref, wo_ref, out_ref):
    x = x_ref[...]
    mod = mod_ref[0]
    hb = _prenorm(x, gain_ref[...], mod, 0).astype(BF16)
    ob = of_ref[...] + or_ref[...]
    g = gb_ref[...]
    parts = [_rms(ob[:, h * LANES:(h + 1) * LANES], float(LANES)) for h in range(N_HEADS_RET)]
    bb = (jnp.concatenate(parts, axis=-1) * _silu(g)).astype(BF16)
    branches = ((oa_ref[...], wa_ref), (bb, wb_ref), (oc_ref[...], wc_ref), (od_ref[...], wd_ref))
    acc = None
    for n, (b, w_ref) in enumerate(branches):
        gate = _sigmoid(_dot(hb, wg_ref[:, n * D_MODEL:(n + 1) * D_MODEL]))
        term = gate * _dot(b, w_ref[...])
        acc = term if acc is None else acc + term
    y = _dot(acc.astype(BF16), wo_ref[...])
    out_ref[...] = x + mod[2:3, :] * y


def _merge_call(xall, modsel, gain, wl, oa, of, orv, gb, oc, od, *, n_rows, n_lat):
    nt = n_rows // ROW_TILE
    row = lambda w: pl.BlockSpec((ROW_TILE, w), lambda i: (i, 0))
    return pl.pallas_call(
        _merge_kernel,
        grid=(nt,),
        in_specs=[row(D_MODEL),
                  _mod_spec(n_lat // ROW_TILE),
                  _const_spec((1, D_MODEL)),
                  _const_spec((D_MODEL, 4 * D_MODEL)),
                  row(1024), row(512), row(512), row(512), row(1024), row(512),
                  _const_spec((1024, D_MODEL)), _const_spec((512, D_MODEL)),
                  _const_spec((1024, D_MODEL)), _const_spec((512, D_MODEL)),
                  _const_spec((D_MODEL, D_MODEL))],
        out_specs=row(D_MODEL),
        out_shape=jax.ShapeDtypeStruct((n_rows, D_MODEL), F32),
        compiler_params=_params(("arbitrary",)),
        name="merge",
    )(xall, modsel, gain, wl["w_gate"], oa, of, orv, gb, oc, od,
      wl["wb_a"], wl["wb_b"], wl["wb_c"], wl["wb_d"], wl["w_out"])


def _ffn_kernel(x_ref, mod_ref, gain_ref, wi_ref, wo_ref, out_ref):
    x = x_ref[...]
    mod = mod_ref[0]
    hb = _prenorm(x, gain_ref[...], mod, 3).astype(BF16)
    gate = _dot(hb, wi_ref[:, :FFN_HIDDEN])
    up = _dot(hb, wi_ref[:, FFN_HIDDEN:])
    y = _dot((_silu(gate) * up).astype(BF16), wo_ref[...])
    out_ref[...] = x + mod[5:6, :] * y


def _ffn_call(x1, modsel, gain, wl, n_lat):
    r = x1.shape[0]
    nt = r // ROW_TILE
    row = pl.BlockSpec((ROW_TILE, D_MODEL), lambda i: (i, 0))
    return pl.pallas_call(
        _ffn_kernel,
        grid=(nt,),
        in_specs=[row, _mod_spec(n_lat // ROW_TILE),
                  _const_spec((1, D_MODEL)),
                  _const_spec((D_MODEL, 2 * FFN_HIDDEN)),
                  _const_spec((FFN_HIDDEN, D_MODEL))],
        out_specs=row,
        out_shape=jax.ShapeDtypeStruct((r, D_MODEL), F32),
        compiler_params=_params(("arbitrary",)),
        name="ffn",
    )(x1, modsel, gain, wl["w_ffn_in"], wl["w_ffn_out"])


def _pad_rows_per_head(w, n_heads, width):
    w = w.reshape(n_heads, width, w.shape[-1])
    w = jnp.pad(w, ((0, 0), (0, LANES - width), (0, 0)))
    return w.reshape(n_heads * LANES, -1)


def _pad_cols_per_head(w, n_heads, width):
    w = w.reshape(w.shape[0], n_heads, width)
    w = jnp.pad(w, ((0, 0), (0, 0), (0, LANES - width)))
    return w.reshape(w.shape[0], n_heads * LANES)


def _pad_vec(v):
    return jnp.pad(v, (0, LANES - v.shape[0]))


def _layer_weights(l, w_in, gqa_qk_gain, mla_cq_gain, mla_ckv_gain, mla_w_uq, mla_w_ukv,
                   mla_qk_gain, diff_qk_gain, w_branch, w_out, w_ffn_in, w_ffn_out):
    wi = w_in[l]
    w_main = jnp.concatenate([wi[:, :2976], jnp.zeros((D_MODEL, 96), F32), wi[:, 2976:4512]], axis=1)
    ukv = mla_w_ukv[l].reshape(256, 8, 128)
    head_gain = jnp.stack([
        _pad_vec(gqa_qk_gain[l, 0]), _pad_vec(gqa_qk_gain[l, 1]),
        _pad_vec(mla_qk_gain[l, 0]), _pad_vec(mla_qk_gain[l, 1]),
        _pad_vec(diff_qk_gain[l, 0]), _pad_vec(diff_qk_gain[l, 1]),
        jnp.zeros((LANES,), F32), jnp.zeros((LANES,), F32)])
    wb = w_branch[l]
    return dict(
        w_main=w_main.astype(BF16),
        w_gate=wi[:, 4512:].astype(BF16),
        wuq=_pad_cols_per_head(mla_w_uq[l], 8, 96).astype(BF16),
        wuk=_pad_cols_per_head(ukv[:, :, :64].reshape(256, 512), 8, 64).astype(BF16),
        wuv=_pad_cols_per_head(ukv[:, :, 64:].reshape(256, 512), 8, 64).astype(BF16),
        head_gain=head_gain,
        cq_gain=mla_cq_gain[l][None], ckv_gain=mla_ckv_gain[l][None],
        wb_a=_pad_rows_per_head(wb[0], 8, 64).astype(BF16),
        wb_b=wb[1].astype(BF16),
        wb_c=_pad_rows_per_head(wb[2], 8, 64).astype(BF16),
        wb_d=wb[3].astype(BF16),
        w_out=w_out[l].astype(BF16),
        w_ffn_in=w_ffn_in[l].astype(BF16),
        w_ffn_out=w_ffn_out[l].astype(BF16),
    )


def _rope_tables(seq, n_rest):
    n_rows = seq // GRID_W
    row = jnp.repeat(jnp.arange(n_rows, dtype=F32), GRID_W)
    col = jnp.tile(jnp.arange(GRID_W, dtype=F32), n_rows)

    def pairs(dim):
        quarter = dim // 4
        freqs = ROPE_THETA ** (-jnp.arange(quarter, dtype=F32) / quarter)
        ang = jnp.concatenate([row[:, None] * freqs, col[:, None] * freqs], axis=-1)
        cos = jnp.repeat(jnp.cos(ang), 2, axis=-1)
        sin = jnp.repeat(jnp.sin(ang), 2, axis=-1) * jnp.tile(jnp.array([-1.0, 1.0], F32), dim // 2)
        return cos, sin

    c64, s64 = pairs(64)
    c32, s32 = pairs(32)
    one = lambda w: jnp.ones((seq, w), F32)
    zero = lambda w: jnp.zeros((seq, w), F32)
    t64c = jnp.concatenate([c64, c64], axis=-1)
    t64s = jnp.concatenate([s64, s64], axis=-1)
    tmc = jnp.concatenate([one(64), c32, one(32)], axis=-1)
    tms = jnp.concatenate([zero(64), s32, zero(32)], axis=-1)
    rest_c = jnp.ones((n_rest, LANES), F32)
    rest_s = jnp.zeros((n_rest, LANES), F32)
    cat = lambda t, rest: jnp.concatenate([t, rest], axis=0)
    return cat(t64c, rest_c), cat(t64s, rest_s), cat(tmc, rest_c), cat(tms, rest_s)


def kernel(x, c, ctx, c_ctx, w_mod, b_mod, norm_gain, w_in, gqa_qk_gain, ret_decay, mla_cq_gain,
           mla_ckv_gain, mla_w_uq, mla_w_ukv, mla_qk_gain, diff_qk_gain, diff_lambda, diff_subln_gain,
           w_branch, w_out, w_ffn_in, w_ffn_out):
    depth = w_mod.shape[0]
    batch, seq, _ = x.shape
    assert batch == 1 and ctx.shape[1] == CTX and seq % (STAGES_PER_TRIP * KV_CHUNK) == 0 and seq % ATT_TQ == 0
    n_all = seq + CTX + PAD

    tabs = _rope_tables(seq, CTX + PAD)
    cc = jnp.concatenate([c_ctx[None], c, jnp.zeros((6, D_MODEL), F32)], axis=0)
    mods = _mod_call(cc, w_mod, b_mod)
    xall = jnp.concatenate([x[0], ctx[0], jnp.zeros((PAD, D_MODEL), F32)], axis=0)

    for l in range(depth):
        last = l == depth - 1
        lam_init = 0.8 - 0.6 * math.exp(-0.3 * l)
        wl = _layer_weights(l, w_in, gqa_qk_gain, mla_cq_gain, mla_ckv_gain, mla_w_uq, mla_w_ukv,
                            mla_qk_gain, diff_qk_gain, w_branch, w_out, w_ffn_in, w_ffn_out)
        modsel = jnp.pad(mods[l, :2].reshape(2, 6, D_MODEL), ((0, 0), (0, 2), (0, 0)))
        g_attn = norm_gain[l, 0][None]
        g_ffn = norm_gain[l, 1][None]

        p = _prep_call(xall, modsel, g_attn, wl, tabs, seq)
        dec = jnp.pad(ret_decay[l].astype(F32), ((0, 6), (0, LANES - N_HEADS_RET)))
        of, orv = _ret_call(p["qb"], p["kb"], p["vb"], dec, seq)
        oa = _attn_call(p["qa"], p["ka"], p["va"], n_heads=8, n_kv=2, nm=1, dv=64, n_lat=seq,
                        name="attn_gqa")
        oc = _attn_call(p["qc"], p["kc"], p["vc"], n_heads=8, n_kv=8, nm=1, dv=64, n_lat=seq,
                        name="attn_mla")
        lam_tab = jnp.pad(diff_lambda[l].astype(F32), ((0, 4), (0, LANES - 64)))
        od = _attn_call(p["qd"], p["kd"], p["vd"], n_heads=4, n_kv=4, nm=2, dv=128, n_lat=seq,
                        name="attn_diff", extra=(lam_tab, diff_subln_gain[l][None]),
                        diff_scale=1.0 - lam_init)
        n_rows = seq if last else n_all
        x1 = _merge_call(xall, modsel, g_attn, wl, oa, of, orv, p["gb"], oc, od, n_rows=n_rows, n_lat=seq)
        xall = _ffn_call(x1, modsel, g_ffn, wl, seq)

    return xall[None]
```

```python
import functools
import math

import jax
import jax.numpy as jnp
from jax import lax
from jax.experimental import pallas as pl
from jax.experimental.pallas import tpu as pltpu

F32 = jnp.float32
BF16 = jnp.bfloat16

D_MODEL = 1024
GRID_W = 64
ROPE_THETA = 10000.0
EPS = 1e-6
RET_CHUNK = 128
N_HEADS_RET = 4
FFN_HIDDEN = 2816
LANES = 128
BF16_ROWS = 16
ROW_TILE = 256
CTX = 256
PAD = 0
ATT_TQ = 2048
Q_BLOCK = 256
KV_CHUNK = 512
STAGES_PER_TRIP = 2
LOG2E = 1.4426950408889634
VMEM_LIMIT = 56 * 1024 * 1024

SEG = dict(a_q=(0, 512), a_k=(512, 640), a_v=(640, 768),
           b_q=(768, 1024), b_k=(1024, 1280), b_v=(1280, 1792), b_g=(1792, 2304),
           c_q=(2304, 2688), c_kv=(2688, 2944), c_kr=(2944, 3072),
           d_q=(3072, 3584), d_k=(3584, 4096), d_v=(4096, 4608))
MAIN_WIDTH = 4608


def _params(sem):
    return pltpu.CompilerParams(dimension_semantics=sem, vmem_limit_bytes=VMEM_LIMIT)


def _const_spec(shape):
    n = len(shape)
    return pl.BlockSpec(shape, lambda *_: (0,) * n, pipeline_mode=pl.Buffered(1))


def _mod_spec(n_lat_tiles):
    return pl.BlockSpec((1, 8, D_MODEL), lambda i: (jnp.where(i < n_lat_tiles, 1, 0), 0, 0))


def _sigmoid(x):
    return 1.0 / (1.0 + jnp.exp(-x))


def _silu(x):
    return x * _sigmoid(x)


def _rms(x, n):
    ss = jnp.sum(x * x, axis=-1, keepdims=True) * (1.0 / n)
    return x * lax.rsqrt(ss + EPS)


def _lane(shape):
    return lax.broadcasted_iota(jnp.int32, shape, 1)


def _rope(y, cos, sin_signed):
    even = (_lane(y.shape) & 1) == 0
    partner = jnp.where(even, pltpu.roll(y, LANES - 1, 1), pltpu.roll(y, 1, 1))
    return y * cos + partner * sin_signed


def _head_slabs(p):
    lo = _lane((p.shape[0], LANES)) < 64
    out = []
    for j in range(p.shape[1] // LANES):
        src = p[:, j * LANES:(j + 1) * LANES]
        out.append(jnp.where(lo, src, 0.0))
        out.append(jnp.where(lo, pltpu.roll(src, 64, 1), 0.0))
    return out


def _dot(a, b):
    return jnp.dot(a, b, preferred_element_type=F32)


def _mod_kernel(c_ref, w_ref, b_ref, o_ref):
    s = _silu(c_ref[...]).astype(BF16)
    o_ref[0] = _dot(s, w_ref[0].astype(BF16)) + b_ref[0]


def _mod_call(cc, w_mod, b_mod):
    depth = w_mod.shape[0]
    nt = 6
    return pl.pallas_call(
        _mod_kernel,
        grid=(depth, nt),
        in_specs=[pl.BlockSpec((8, D_MODEL), lambda l, j: (0, 0)),
                  pl.BlockSpec((1, D_MODEL, D_MODEL), lambda l, j: (l, 0, j)),
                  pl.BlockSpec((1, 1, D_MODEL), lambda l, j: (l, 0, j))],
        out_specs=pl.BlockSpec((1, 8, D_MODEL), lambda l, j: (l, 0, j)),
        out_shape=jax.ShapeDtypeStruct((depth, 8, 6 * D_MODEL), F32),
        compiler_params=_params(("arbitrary", "arbitrary")),
        name="mod",
    )(cc, w_mod, b_mod.reshape(depth, 1, 6 * D_MODEL))


def _prenorm(x, gain, mod, shift_row):
    h = _rms(x, D_MODEL) * gain
    return h * (1.0 + mod[shift_row + 1:shift_row + 2, :]) + mod[shift_row:shift_row + 1, :]


def _prep_kernel(x_ref, mod_ref, gain_ref, w_ref, wuq_ref, wuk_ref, wuv_ref,
                 hg_ref, cqg_ref, ckvg_ref, c64_ref, s64_ref, cm_ref, sm_ref,
                 qa_ref, ka_ref, va_ref, qb_ref, kb_ref, vb_ref, gb_ref,
                 qc_ref, kc_ref, vc_ref, qd_ref, kd_ref, vd_ref):
    hb = _prenorm(x_ref[...], gain_ref[...], mod_ref[0], 0).astype(BF16)
    t = hb.shape[0]
    lane = _lane((t, LANES))
    c64, s64 = c64_ref[...], s64_ref[...]
    cm, sm = cm_ref[...], sm_ref[...]

    def proj(name):
        a, b = SEG[name]
        return _dot(hb, w_ref[:, a:b])

    def put(ref, j, val):
        ref[:, j * LANES:(j + 1) * LANES] = val.astype(ref.dtype)

    def rms64(slab):
        ss = jnp.sum(slab * slab, axis=-1, keepdims=True) * (1.0 / 64.0)
        return slab * lax.rsqrt(ss + EPS)

    def nr64(slab, gain_row, scale):
        y = rms64(slab) * hg_ref[gain_row:gain_row + 1, :]
        y = _rope(y, c64, s64)
        return y * scale if scale != 1.0 else y

    ones_at_64 = lane == 64

    for j, slab in enumerate(_head_slabs(proj("a_q"))):
        put(qa_ref, j, nr64(slab, 0, 64 ** -0.5 * LOG2E))
    for j, slab in enumerate(_head_slabs(proj("a_k"))):
        put(ka_ref, j, nr64(slab, 1, 1.0))
    for j, slab in enumerate(_head_slabs(proj("a_v"))):
        put(va_ref, j, jnp.where(ones_at_64, 1.0, slab))

    q = proj("b_q")
    k = proj("b_k") * (64 ** -0.5)
    for j in range(2):
        put(qb_ref, j, _rope(q[:, j * LANES:(j + 1) * LANES], c64, s64))
        put(kb_ref, j, _rope(k[:, j * LANES:(j + 1) * LANES], c64, s64))
    vb_ref[...] = proj("b_v").astype(vb_ref.dtype)
    gb_ref[...] = proj("b_g")

    nope = lane < 64
    ropel = (lane >= 64) & (lane < 96)
    cq = (_rms(proj("c_q"), 384.0) * cqg_ref[...]).astype(BF16)
    qm = _dot(cq, wuq_ref[...])
    gq = hg_ref[2:3, :]
    for h in range(8):
        slab = qm[:, h * LANES:(h + 1) * LANES]
        sq = slab * slab
        ssn = jnp.sum(jnp.where(nope, sq, 0.0), axis=-1, keepdims=True) * (1.0 / 64.0)
        ssr = jnp.sum(jnp.where(ropel, sq, 0.0), axis=-1, keepdims=True) * (1.0 / 32.0)
        inv = jnp.where(nope, lax.rsqrt(ssn + EPS), lax.rsqrt(ssr + EPS))
        y = _rope(slab * inv * gq, cm, sm) * (96 ** -0.5 * LOG2E)
        put(qc_ref, h, y)
    ckv = (_rms(proj("c_kv"), 256.0) * ckvg_ref[...]).astype(BF16)
    kn = _dot(ckv, wuk_ref[...])
    vm = _dot(ckv, wuv_ref[...])
    gk = hg_ref[3:4, :]
    kr = pltpu.roll(proj("c_kr"), 64, 1)
    ssk = jnp.sum(kr * kr, axis=-1, keepdims=True) * (1.0 / 32.0)
    kr = _rope(kr * lax.rsqrt(ssk + EPS) * gk, cm, sm)
    kr = jnp.where(ropel, kr, 0.0)
    for h in range(8):
        slab = kn[:, h * LANES:(h + 1) * LANES]
        put(kc_ref, h, jnp.where(nope, rms64(slab) * gk, kr))
        put(vc_ref, h, jnp.where(ones_at_64, 1.0, vm[:, h * LANES:(h + 1) * LANES]))

    for j, slab in enumerate(_head_slabs(proj("d_q"))):
        put(qd_ref, j, nr64(slab, 4, 64 ** -0.5 * LOG2E))
    for j, slab in enumerate(_head_slabs(proj("d_k"))):
        put(kd_ref, j, nr64(slab, 5, 1.0))
    v = proj("d_v")
    ones_at_0 = jnp.where(lane == 0, 1.0, 0.0)
    for h in range(4):
        put(vd_ref, 2 * h, v[:, h * LANES:(h + 1) * LANES])
        put(vd_ref, 2 * h + 1, ones_at_0)


def _prep_call(xall, modsel, gain, wl, tabs, n_lat):
    r = xall.shape[0]
    nt = r // ROW_TILE
    row = lambda w: pl.BlockSpec((ROW_TILE, w), lambda i: (i, 0))
    out_w = dict(qa=1024, ka=256, va=256, qb=256, kb=256, vb=512, gb=512,
                 qc=1024, kc=1024, vc=1024, qd=1024, kd=1024, vd=1024)
    out_shape = [jax.ShapeDtypeStruct((r, w), F32 if n == "gb" else BF16) for n, w in out_w.items()]
    outs = pl.pallas_call(
        _prep_kernel,
        grid=(nt,),
        in_specs=[row(D_MODEL),
                  _mod_spec(n_lat // ROW_TILE),
                  _const_spec((1, D_MODEL)),
                  _const_spec((D_MODEL, MAIN_WIDTH)),
                  _const_spec((384, 1024)), _const_spec((256, 1024)), _const_spec((256, 1024)),
                  _const_spec((8, LANES)), _const_spec((1, 384)), _const_spec((1, 256)),
                  row(LANES), row(LANES), row(LANES), row(LANES)],
        out_specs=[row(w) for w in out_w.values()],
        out_shape=out_shape,
        compiler_params=_params(("arbitrary",)),
        name="prep",
    )(xall, modsel, gain, wl["w_main"], wl["wuq"], wl["wuk"], wl["wuv"],
      wl["head_gain"], wl["cq_gain"], wl["ckv_gain"], *tabs)
    return dict(zip(out_w.keys(), outs))


N_PAD_CHUNKS = PAD // RET_CHUNK


def _ret_kernel(dec_ref, qf_ref, kf_ref, vf_ref, qr_ref, kr_ref, vr_ref,
                of_ref, or_ref, state_ref, intra_ref, qdec_ref, kdec_ref, cdec_ref):
    c = RET_CHUNK
    t = pl.program_id(0)

    @pl.when(t == 0)
    def _tables():
        x = dec_ref[...]
        lg = jnp.minimum(x, 0.0) - jnp.log(1.0 + jnp.exp(-jnp.abs(x)))
        i_idx = lax.broadcasted_iota(jnp.int32, (c, c), 0).astype(F32)
        j_idx = lax.broadcasted_iota(jnp.int32, (c, c), 1).astype(F32)
        for d in range(2):
            for h in range(N_HEADS_RET):
                g = lg[d:d + 1, h:h + 1]
                if d == 0:
                    rel, qe, ke = i_idx - j_idx, i_idx + 1.0, c - 1.0 - i_idx
                else:
                    rel, qe, ke = j_idx - i_idx, c - i_idx, i_idx
                intra_ref[d, h] = jnp.where(rel >= 0, jnp.exp(g * jnp.maximum(rel, 0.0)), 0.0)
                qdec_ref[d, h] = jnp.exp(g * qe)
                kdec_ref[d, h] = jnp.exp(g * ke[:, :64])
                cdec_ref[d, h] = jnp.exp(jnp.broadcast_to(g, (8, LANES)) * float(c))

    @pl.when((t == 0) | (t == N_PAD_CHUNKS))
    def _clear():
        state_ref[...] = jnp.zeros_like(state_ref)

    for d, (q_ref, k_ref, v_ref, o_ref) in enumerate(
            ((qf_ref, kf_ref, vf_ref, of_ref), (qr_ref, kr_ref, vr_ref, or_ref))):
        for h in range(N_HEADS_RET):
            qh = q_ref[:, h * 64:(h + 1) * 64]
            kh = k_ref[:, h * 64:(h + 1) * 64]
            vh = v_ref[:, h * LANES:(h + 1) * LANES]
            sc = lax.dot_general(qh, kh, (((1,), (1,)), ((), ())), preferred_element_type=F32)
            sc = sc * intra_ref[d, h]
            st = state_ref[d, h]
            o = _dot(sc.astype(BF16), vh) + _dot(qh, st.astype(BF16)) * qdec_ref[d, h]
            o_ref[:, h * LANES:(h + 1) * LANES] = o
            kd = (kh.astype(F32) * kdec_ref[d, h]).astype(BF16)
            upd = lax.dot_general(kd, vh, (((0,), (0,)), ((), ())), preferred_element_type=F32)
            state_ref[d, h] = st * cdec_ref[d, h][0:1, :] + upd


def _ret_call(qb, kb, vb, dec, n_lat):
    r = qb.shape[0]
    nc = r // RET_CHUNK
    nl = n_lat // RET_CHUNK
    nctx = CTX // RET_CHUNK
    npad = N_PAD_CHUNKS
    assert nc == nl + nctx + npad
    fwd = lambda t: (jnp.where(t < npad, nl + nctx + t,
                               jnp.where(t < npad + nctx, nl + t - npad, t - npad - nctx)), 0)
    bwd = lambda t: (jnp.where(t < npad, nl + nctx + t, nc - 1 - t), 0)
    blk = lambda w, m: pl.BlockSpec((RET_CHUNK, w), m)
    return pl.pallas_call(
        _ret_kernel,
        grid=(nc,),
        in_specs=[_const_spec((8, LANES)),
                  blk(256, fwd), blk(256, fwd), blk(512, fwd),
                  blk(256, bwd), blk(256, bwd), blk(512, bwd)],
        out_specs=[blk(512, fwd), blk(512, bwd)],
        out_shape=[jax.ShapeDtypeStruct((r, 512), F32)] * 2,
        scratch_shapes=[pltpu.VMEM((2, N_HEADS_RET, 64, LANES), F32),
                        pltpu.VMEM((2, N_HEADS_RET, RET_CHUNK, RET_CHUNK), F32),
                        pltpu.VMEM((2, N_HEADS_RET, RET_CHUNK, RET_CHUNK), F32),
                        pltpu.VMEM((2, N_HEADS_RET, RET_CHUNK, 64), F32),
                        pltpu.VMEM((2, N_HEADS_RET, 8, LANES), F32)],
        compiler_params=_params(("arbitrary",)),
        name="retention",
    )(dec, qb, kb, vb, qb, kb, vb)


def _attn_kernel(*refs, nm, dv, n_lat, diff_scale):
    if diff_scale is None:
        qt_ref, k_ref, vl_ref, vc_ref, o_ref, m_ref, acc_ref, s_ref, p_ref, mx_ref, alpha_ref = refs
    else:
        qt_ref, k_ref, vl_ref, vc_ref, lam_ref, sg_ref, o_ref, m_ref, acc_ref, s_ref, p_ref, mx_ref, alpha_ref = refs
    i = pl.program_id(1)
    n_chunks = n_lat // KV_CHUNK
    tq = qt_ref.shape[1]
    maps = range(nm)

    def scores(k, a):
        return _dot(k[:, a * LANES:(a + 1) * LANES], qt_ref[a * LANES:(a + 1) * LANES, :])

    kc = k_ref[n_lat:n_lat + CTX, :]
    for a in maps:
        s = scores(kc, a)
        m = jnp.max(s, axis=0, keepdims=True)
        m_ref[a] = m
        acc_ref[a] = _dot(vc_ref[0], jnp.exp2(s - m).astype(BF16))

    def qk(j, slot):
        k = k_ref[pl.ds(pl.multiple_of(j * KV_CHUNK, KV_CHUNK), KV_CHUNK), :]
        for a in maps:
            s = scores(k, a)
            s_ref[a, slot] = s
            mx_ref[a, slot] = jnp.max(s, axis=0, keepdims=True)

    @pl.when(i < n_lat // tq)
    def _latent_keys():
        qk(0, 0)
        p_ref[:, 1] = jnp.zeros((nm,) + p_ref.shape[2:], BF16)
        alpha_ref[...] = jnp.ones_like(alpha_ref)

        def stage(j, slot, prefetch=True, softmax=True):
            other = 1 - slot
            if isinstance(j, int):
                j_prev, k_start = max(j - 1, 0), (j + 1) * KV_CHUNK
            else:
                j_prev, k_start = jnp.maximum(j - 1, 0), pl.multiple_of((j + 1) * KV_CHUNK, KV_CHUNK)
            k_rows = pl.ds(k_start, KV_CHUNK)
            for c in range(tq // Q_BLOCK):
                cols = slice(c * Q_BLOCK, (c + 1) * Q_BLOCK)
                for a in maps:
                    heads = slice(a * LANES, (a + 1) * LANES)
                    if softmax:
                        m_old = m_ref[a, :, cols]
                        m_new = jnp.maximum(m_old, mx_ref[a, slot, :, cols])
                        p_ref[a, slot, :, cols] = jnp.exp2(s_ref[a, slot, :, cols] - m_new).astype(BF16)
                        m_ref[a, :, cols] = m_new
                    pv_c = _dot(vl_ref[0, j_prev], p_ref[a, other, :, cols])
                    if prefetch:
                        s = _dot(k_ref[k_rows, heads], qt_ref[heads, cols])
                        s_ref[a, other, :, cols] = s
                        mx_ref[a, other, :, cols] = jnp.max(s, axis=0, keepdims=True)
                    acc_ref[a, :, cols] = acc_ref[a, :, cols] * alpha_ref[a, :, cols] + pv_c
                    if softmax:
                        alpha_ref[a, :, cols] = jnp.exp2(m_old - m_new)

        def trip(t, carry):
            for u in range(STAGES_PER_TRIP):
                stage(STAGES_PER_TRIP * t + u, u % 2)
            return carry

        n_trips = n_chunks // STAGES_PER_TRIP - 1
        lax.fori_loop(0, n_trips, trip, 0)
        for j in range(n_trips * STAGES_PER_TRIP, n_chunks):
            stage(j, j % 2, prefetch=j + 1 < n_chunks)
        stage(n_chunks, n_chunks % 2, prefetch=False, softmax=False)

    def normalised(a):
        acc = acc_ref[a]
        return acc[:dv] * (1.0 / acc[dv:dv + 1])

    if diff_scale is None:
        o = normalised(0)
        o = jnp.concatenate([o, jnp.zeros((LANES - dv, o.shape[1]), F32)], axis=0)
        o_ref[...] = o.T.astype(o_ref.dtype)
    else:
        lv = lam_ref[...]
        lam_init = 1.0 - diff_scale
        lam = (jnp.exp(jnp.sum(lv[0:1] * lv[1:2], axis=-1, keepdims=True))
               - jnp.exp(jnp.sum(lv[2:3] * lv[3:4], axis=-1, keepdims=True)) + lam_init)
        o = (normalised(0) - lam * normalised(1)).T
        o_ref[...] = (_rms(o, float(dv)) * sg_ref[...] * diff_scale).astype(o_ref.dtype)


def _attn_call(q, k, v, *, n_heads, n_kv, nm, dv, n_lat, tq, name, extra=(), diff_scale=None):
    r = q.shape[0]
    assert n_lat % tq == 0 and r - n_lat == CTX <= tq
    nq = n_lat // tq + 1
    group = n_heads // n_kv
    dvp = -(-(dv + 1) // BF16_ROWS) * BF16_ROWS
    slab = v.shape[1] // n_kv
    n_chunks = n_lat // KV_CHUNK
    qt = jnp.pad(q.T, ((0, 0), (0, nq * tq - r)))
    v3 = v.reshape(r, n_kv, slab)[:, :, :dvp]
    vl = v3[:n_lat].reshape(n_chunks, KV_CHUNK, n_kv, dvp).transpose(2, 0, 3, 1)
    vc = v3[n_lat:].transpose(1, 2, 0)
    kern = functools.partial(_attn_kernel, nm=nm, dv=dv, n_lat=n_lat, diff_scale=diff_scale)
    extra_specs = [pl.BlockSpec(e.shape, lambda h, i: (0, 0)) for e in extra]
    once = pl.Buffered(1)
    return pl.pallas_call(
        kern,
        grid=(n_heads, nq),
        in_specs=[pl.BlockSpec((nm * LANES, tq), lambda h, i: (h, i)),
                  pl.BlockSpec((r, nm * LANES), lambda h, i: (0, h // group), pipeline_mode=once),
                  pl.BlockSpec((1, n_chunks, dvp, KV_CHUNK), lambda h, i: (h // group, 0, 0, 0),
                               pipeline_mode=once),
                  pl.BlockSpec((1, dvp, CTX), lambda h, i: (h // group, 0, 0))] + extra_specs,
        out_specs=pl.BlockSpec((tq, LANES), lambda h, i: (i, h)),
        out_shape=jax.ShapeDtypeStruct((nq * tq, n_heads * LANES), BF16),
        scratch_shapes=[pltpu.VMEM((nm, 1, tq), F32),
                        pltpu.VMEM((nm, dvp, tq), F32),
                        pltpu.VMEM((nm, 2, KV_CHUNK, tq), F32),
                        pltpu.VMEM((nm, 2, KV_CHUNK, tq), BF16),
                        pltpu.VMEM((nm, 2, 1, tq), F32),
                        pltpu.VMEM((nm, 1, tq), F32)],
        compiler_params=_params(("arbitrary", "arbitrary")),
        name=name,
    )(qt, k, vl, vc, *extra)


def _merge_kernel(x_ref, mod_ref, gain_ref, wg_ref, oa_ref, of_ref, or_ref, gb_ref, oc_ref, od_ref,
                  wa_ref, wb_ref, wc_ref, wd_ref, wo_ref, out_ref):
    x = x_ref[...]
    mod = mod_ref[0]
    hb = _prenorm(x, gain_ref[...], mod, 0).astype(BF16)
    ob = of_ref[...] + or_ref[...]
    g = gb_ref[...]
    parts = [_rms(ob[:, h * LANES:(h + 1) * LANES], float(LANES)) for h in range(N_HEADS_RET)]
    bb = (jnp.concatenate(parts, axis=-1) * _silu(g)).astype(BF16)
    branches = ((oa_ref[...], wa_ref), (bb, wb_ref), (oc_ref[...], wc_ref), (od_ref[...], wd_ref))
    acc = None
    for n, (b, w_ref) in enumerate(branches):
        gate = _sigmoid(_dot(hb, wg_ref[:, n * D_MODEL:(n + 1) * D_MODEL]))
        term = gate * _dot(b, w_ref[...])
        acc = term if acc is None else acc + term
    y = _dot(acc.astype(BF16), wo_ref[...])
    out_ref[...] = x + mod[2:3, :] * y


def _merge_call(xall, modsel, gain, wl, oa, of, orv, gb, oc, od, *, n_rows, n_lat):
    nt = n_rows // ROW_TILE
    row = lambda w: pl.BlockSpec((ROW_TILE, w), lambda i: (i, 0))
    return pl.pallas_call(
        _merge_kernel,
        grid=(nt,),
        in_specs=[row(D_MODEL),
                  _mod_spec(n_lat // ROW_TILE),
                  _const_spec((1, D_MODEL)),
                  _const_spec((D_MODEL, 4 * D_MODEL)),
                  row(1024), row(512), row(512), row(512), row(1024), row(512),
                  _const_spec((1024, D_MODEL)), _const_spec((512, D_MODEL)),
                  _const_spec((1024, D_MODEL)), _const_spec((512, D_MODEL)),
                  _const_spec((D_MODEL, D_MODEL))],
        out_specs=row(D_MODEL),
        out_shape=jax.ShapeDtypeStruct((n_rows, D_MODEL), F32),
        compiler_params=_params(("arbitrary",)),
        name="merge",
    )(xall, modsel, gain, wl["w_gate"], oa, of, orv, gb, oc, od,
      wl["wb_a"], wl["wb_b"], wl["wb_c"], wl["wb_d"], wl["w_out"])


def _ffn_kernel(x_ref, mod_ref, gain_ref, wi_ref, wo_ref, out_ref):
    x = x_ref[...]
    mod = mod_ref[0]
    hb = _prenorm(x, gain_ref[...], mod, 3).astype(BF16)
    gate = _dot(hb, wi_ref[:, :FFN_HIDDEN])
    up = _dot(hb, wi_ref[:, FFN_HIDDEN:])
    y = _dot((_silu(gate) * up).astype(BF16), wo_ref[...])
    out_ref[...] = x + mod[5:6, :] * y


def _ffn_call(x1, modsel, gain, wl, n_lat):
    r = x1.shape[0]
    nt = r // ROW_TILE
    row = pl.BlockSpec((ROW_TILE, D_MODEL), lambda i: (i, 0))
    return pl.pallas_call(
        _ffn_kernel,
        grid=(nt,),
        in_specs=[row, _mod_spec(n_lat // ROW_TILE),
                  _const_spec((1, D_MODEL)),
                  _const_spec((D_MODEL, 2 * FFN_HIDDEN)),
                  _const_spec((FFN_HIDDEN, D_MODEL))],
        out_specs=row,
        out_shape=jax.ShapeDtypeStruct((r, D_MODEL), F32),
        compiler_params=_params(("arbitrary",)),
        name="ffn",
    )(x1, modsel, gain, wl["w_ffn_in"], wl["w_ffn_out"])


def _pad_rows_per_head(w, n_heads, width):
    w = w.reshape(n_heads, width, w.shape[-1])
    w = jnp.pad(w, ((0, 0), (0, LANES - width), (0, 0)))
    return w.reshape(n_heads * LANES, -1)


def _pad_cols_per_head(w, n_heads, width):
    w = w.reshape(w.shape[0], n_heads, width)
    w = jnp.pad(w, ((0, 0), (0, 0), (0, LANES - width)))
    return w.reshape(w.shape[0], n_heads * LANES)


def _pad_vec(v):
    return jnp.pad(v, (0, LANES - v.shape[0]))


def _layer_weights(l, w_in, gqa_qk_gain, mla_cq_gain, mla_ckv_gain, mla_w_uq, mla_w_ukv,
                   mla_qk_gain, diff_qk_gain, w_branch, w_out, w_ffn_in, w_ffn_out):
    wi = w_in[l]
    w_main = jnp.concatenate([wi[:, :2976], jnp.zeros((D_MODEL, 96), F32), wi[:, 2976:4512]], axis=1)
    ukv = mla_w_ukv[l].reshape(256, 8, 128)
    head_gain = jnp.stack([
        _pad_vec(gqa_qk_gain[l, 0]), _pad_vec(gqa_qk_gain[l, 1]),
        _pad_vec(mla_qk_gain[l, 0]), _pad_vec(mla_qk_gain[l, 1]),
        _pad_vec(diff_qk_gain[l, 0]), _pad_vec(diff_qk_gain[l, 1]),
        jnp.zeros((LANES,), F32), jnp.zeros((LANES,), F32)])
    wb = w_branch[l]
    return dict(
        w_main=w_main.astype(BF16),
        w_gate=wi[:, 4512:].astype(BF16),
        wuq=_pad_cols_per_head(mla_w_uq[l], 8, 96).astype(BF16),
        wuk=_pad_cols_per_head(ukv[:, :, :64].reshape(256, 512), 8, 64).astype(BF16),
        wuv=_pad_cols_per_head(ukv[:, :, 64:].reshape(256, 512), 8, 64).astype(BF16),
        head_gain=head_gain,
        cq_gain=mla_cq_gain[l][None], ckv_gain=mla_ckv_gain[l][None],
        wb_a=_pad_rows_per_head(wb[0], 8, 64).astype(BF16),
        wb_b=wb[1].astype(BF16),
        wb_c=_pad_rows_per_head(wb[2], 8, 64).astype(BF16),
        wb_d=wb[3].astype(BF16),
        w_out=w_out[l].astype(BF16),
        w_ffn_in=w_ffn_in[l].astype(BF16),
        w_ffn_out=w_ffn_out[l].astype(BF16),
    )


def _rope_tables(seq, n_rest):
    n_rows = seq // GRID_W
    row = jnp.repeat(jnp.arange(n_rows, dtype=F32), GRID_W)
    col = jnp.tile(jnp.arange(GRID_W, dtype=F32), n_rows)

    def pairs(dim):
        quarter = dim // 4
        freqs = ROPE_THETA ** (-jnp.arange(quarter, dtype=F32) / quarter)
        ang = jnp.concatenate([row[:, None] * freqs, col[:, None] * freqs], axis=-1)
        cos = jnp.repeat(jnp.cos(ang), 2, axis=-1)
        sin = jnp.repeat(jnp.sin(ang), 2, axis=-1) * jnp.tile(jnp.array([-1.0, 1.0], F32), dim // 2)
        return cos, sin

    c64, s64 = pairs(64)
    c32, s32 = pairs(32)
    one = lambda w: jnp.ones((seq, w), F32)
    zero = lambda w: jnp.zeros((seq, w), F32)
    t64c = jnp.concatenate([c64, c64], axis=-1)
    t64s = jnp.concatenate([s64, s64], axis=-1)
    tmc = jnp.concatenate([one(64), c32, one(32)], axis=-1)
    tms = jnp.concatenate([zero(64), s32, zero(32)], axis=-1)
    rest_c = jnp.ones((n_rest, LANES), F32)
    rest_s = jnp.zeros((n_rest, LANES), F32)
    cat = lambda t, rest: jnp.concatenate([t, rest], axis=0)
    return cat(t64c, rest_c), cat(t64s, rest_s), cat(tmc, rest_c), cat(tms, rest_s)


def kernel(x, c, ctx, c_ctx, w_mod, b_mod, norm_gain, w_in, gqa_qk_gain, ret_decay, mla_cq_gain,
           mla_ckv_gain, mla_w_uq, mla_w_ukv, mla_qk_gain, diff_qk_gain, diff_lambda, diff_subln_gain,
           w_branch, w_out, w_ffn_in, w_ffn_out):
    depth = w_mod.shape[0]
    batch, seq, _ = x.shape
    assert batch == 1 and ctx.shape[1] == CTX and seq % (STAGES_PER_TRIP * KV_CHUNK) == 0
    n_all = seq + CTX + PAD

    tabs = _rope_tables(seq, CTX + PAD)
    cc = jnp.concatenate([c_ctx[None], c, jnp.zeros((6, D_MODEL), F32)], axis=0)
    mods = _mod_call(cc, w_mod, b_mod)
    xall = jnp.concatenate([x[0], ctx[0], jnp.zeros((PAD, D_MODEL), F32)], axis=0)

    for l in range(depth):
        last = l == depth - 1
        lam_init = 0.8 - 0.6 * math.exp(-0.3 * l)
        wl = _layer_weights(l, w_in, gqa_qk_gain, mla_cq_gain, mla_ckv_gain, mla_w_uq, mla_w_ukv,
                            mla_qk_gain, diff_qk_gain, w_branch, w_out, w_ffn_in, w_ffn_out)
        modsel = jnp.pad(mods[l, :2].reshape(2, 6, D_MODEL), ((0, 0), (0, 2), (0, 0)))
        g_attn = norm_gain[l, 0][None]
        g_ffn = norm_gain[l, 1][None]

        p = _prep_call(xall, modsel, g_attn, wl, tabs, seq)
        dec = jnp.pad(ret_decay[l].astype(F32), ((0, 6), (0, LANES - N_HEADS_RET)))
        of, orv = _ret_call(p["qb"], p["kb"], p["vb"], dec, seq)
        oa = _attn_call(p["qa"], p["ka"], p["va"], n_heads=8, n_kv=2, nm=1, dv=64, n_lat=seq,
                        tq=ATT_TQ, name="attn_gqa")
        oc = _attn_call(p["qc"], p["kc"], p["vc"], n_heads=8, n_kv=8, nm=1, dv=64, n_lat=seq,
                        tq=ATT_TQ, name="attn_mla")
        lam_tab = jnp.pad(diff_lambda[l].astype(F32), ((0, 4), (0, LANES - 64)))
        od = _attn_call(p["qd"], p["kd"], p["vd"], n_heads=4, n_kv=4, nm=2, dv=128, n_lat=seq,
                        tq=ATT_TQ, name="attn_diff", extra=(lam_tab, diff_subln_gain[l][None]),
                        diff_scale=1.0 - lam_init)
        n_rows = seq if last else n_all
        x1 = _merge_call(xall, modsel, g_attn, wl, oa, of, orv, p["gb"], oc, od, n_rows=n_rows, n_lat=seq)
        xall = _ffn_call(x1, modsel, g_ffn, wl, seq)

    return xall[None]
```

```python
import functools
import math

import jax
import jax.numpy as jnp
from jax import lax
from jax.experimental import pallas as pl
from jax.experimental.pallas import tpu as pltpu

F32 = jnp.float32
BF16 = jnp.bfloat16

D_MODEL = 1024
GRID_W = 64
ROPE_THETA = 10000.0
EPS = 1e-6
RET_CHUNK = 128
N_HEADS_RET = 4
FFN_HIDDEN = 2816
LANES = 128
BF16_ROWS = 16
ROW_TILE = 256
CTX = 256
PAD = 0
ATT_TQ = 2048
Q_BLOCK = 256
KV_CHUNK = 512
STAGES_PER_TRIP = 2
LOG2E = 1.4426950408889634
SHIFT_MAX = 40.0
VMEM_LIMIT = 56 * 1024 * 1024

SEG = dict(a_q=(0, 512), a_k=(512, 640), a_v=(640, 768),
           b_q=(768, 1024), b_k=(1024, 1280), b_v=(1280, 1792), b_g=(1792, 2304),
           c_q=(2304, 2688), c_kv=(2688, 2944), c_kr=(2944, 3072),
           d_q=(3072, 3584), d_k=(3584, 4096), d_v=(4096, 4608))
MAIN_WIDTH = 4608


def _params(sem):
    return pltpu.CompilerParams(dimension_semantics=sem, vmem_limit_bytes=VMEM_LIMIT)


def _const_spec(shape):
    n = len(shape)
    return pl.BlockSpec(shape, lambda *_: (0,) * n, pipeline_mode=pl.Buffered(1))


def _mod_spec(n_lat_tiles):
    return pl.BlockSpec((1, 8, D_MODEL), lambda i: (jnp.where(i < n_lat_tiles, 1, 0), 0, 0))


def _sigmoid(x):
    return 1.0 / (1.0 + jnp.exp(-x))


def _silu(x):
    return x * _sigmoid(x)


def _rms(x, n):
    ss = jnp.sum(x * x, axis=-1, keepdims=True) * (1.0 / n)
    return x * lax.rsqrt(ss + EPS)


def _lane(shape):
    return lax.broadcasted_iota(jnp.int32, shape, 1)


def _rope(y, cos, sin_signed):
    even = (_lane(y.shape) & 1) == 0
    partner = jnp.where(even, pltpu.roll(y, LANES - 1, 1), pltpu.roll(y, 1, 1))
    return y * cos + partner * sin_signed


def _head_slabs(p):
    lo = _lane((p.shape[0], LANES)) < 64
    out = []
    for j in range(p.shape[1] // LANES):
        src = p[:, j * LANES:(j + 1) * LANES]
        out.append(jnp.where(lo, src, 0.0))
        out.append(jnp.where(lo, pltpu.roll(src, 64, 1), 0.0))
    return out


def _dot(a, b):
    return jnp.dot(a, b, preferred_element_type=F32)


def _mod_kernel(c_ref, w_ref, b_ref, o_ref):
    s = _silu(c_ref[...]).astype(BF16)
    o_ref[0] = _dot(s, w_ref[0].astype(BF16)) + b_ref[0]


def _mod_call(cc, w_mod, b_mod):
    depth = w_mod.shape[0]
    nt = 6
    return pl.pallas_call(
        _mod_kernel,
        grid=(depth, nt),
        in_specs=[pl.BlockSpec((8, D_MODEL), lambda l, j: (0, 0)),
                  pl.BlockSpec((1, D_MODEL, D_MODEL), lambda l, j: (l, 0, j)),
                  pl.BlockSpec((1, 1, D_MODEL), lambda l, j: (l, 0, j))],
        out_specs=pl.BlockSpec((1, 8, D_MODEL), lambda l, j: (l, 0, j)),
        out_shape=jax.ShapeDtypeStruct((depth, 8, 6 * D_MODEL), F32),
        compiler_params=_params(("arbitrary", "arbitrary")),
        name="mod",
    )(cc, w_mod, b_mod.reshape(depth, 1, 6 * D_MODEL))


def _prenorm(x, gain, mod, shift_row):
    h = _rms(x, D_MODEL) * gain
    return h * (1.0 + mod[shift_row + 1:shift_row + 2, :]) + mod[shift_row:shift_row + 1, :]


def _prep_kernel(x_ref, mod_ref, gain_ref, w_ref, wuq_ref, wuk_ref, wuv_ref,
                 hg_ref, shift_ref, cqg_ref, ckvg_ref, c64_ref, s64_ref, cm_ref, sm_ref,
                 qa_ref, ka_ref, va_ref, qb_ref, kb_ref, vb_ref, gb_ref,
                 qc_ref, kc_ref, vc_ref, qd_ref, kd_ref, vd_ref):
    hb = _prenorm(x_ref[...], gain_ref[...], mod_ref[0], 0).astype(BF16)
    t = hb.shape[0]
    lane = _lane((t, LANES))
    c64, s64 = c64_ref[...], s64_ref[...]
    cm, sm = cm_ref[...], sm_ref[...]

    def proj(name):
        a, b = SEG[name]
        return _dot(hb, w_ref[:, a:b])

    def put(ref, j, val):
        ref[:, j * LANES:(j + 1) * LANES] = val.astype(ref.dtype)

    def rms64(slab):
        ss = jnp.sum(slab * slab, axis=-1, keepdims=True) * (1.0 / 64.0)
        return slab * lax.rsqrt(ss + EPS)

    def nr64(slab, gain_row, scale):
        y = rms64(slab) * hg_ref[gain_row:gain_row + 1, :]
        y = _rope(y, c64, s64)
        return y * scale if scale != 1.0 else y

    ones_at_64 = lane == 64
    ones_at_96 = lane == 96

    for j, slab in enumerate(_head_slabs(proj("a_q"))):
        put(qa_ref, j, jnp.where(ones_at_64, shift_ref[0:1, :], nr64(slab, 0, 64 ** -0.5 * LOG2E)))
    for j, slab in enumerate(_head_slabs(proj("a_k"))):
        put(ka_ref, j, jnp.where(ones_at_64, 1.0, nr64(slab, 1, 1.0)))
    for j, slab in enumerate(_head_slabs(proj("a_v"))):
        put(va_ref, j, jnp.where(ones_at_64, 1.0, slab))

    q = proj("b_q")
    k = proj("b_k") * (64 ** -0.5)
    for j in range(2):
        put(qb_ref, j, _rope(q[:, j * LANES:(j + 1) * LANES], c64, s64))
        put(kb_ref, j, _rope(k[:, j * LANES:(j + 1) * LANES], c64, s64))
    vb_ref[...] = proj("b_v").astype(vb_ref.dtype)
    gb_ref[...] = proj("b_g")

    nope = lane < 64
    ropel = (lane >= 64) & (lane < 96)
    cq = (_rms(proj("c_q"), 384.0) * cqg_ref[...]).astype(BF16)
    qm = _dot(cq, wuq_ref[...])
    gq = hg_ref[2:3, :]
    for h in range(8):
        slab = qm[:, h * LANES:(h + 1) * LANES]
        sq = slab * slab
        ssn = jnp.sum(jnp.where(nope, sq, 0.0), axis=-1, keepdims=True) * (1.0 / 64.0)
        ssr = jnp.sum(jnp.where(ropel, sq, 0.0), axis=-1, keepdims=True) * (1.0 / 32.0)
        inv = jnp.where(nope, lax.rsqrt(ssn + EPS), lax.rsqrt(ssr + EPS))
        y = _rope(slab * inv * gq, cm, sm) * (96 ** -0.5 * LOG2E)
        put(qc_ref, h, jnp.where(ones_at_96, shift_ref[1:2, :], y))
    ckv = (_rms(proj("c_kv"), 256.0) * ckvg_ref[...]).astype(BF16)
    kn = _dot(ckv, wuk_ref[...])
    vm = _dot(ckv, wuv_ref[...])
    gk = hg_ref[3:4, :]
    kr = pltpu.roll(proj("c_kr"), 64, 1)
    ssk = jnp.sum(kr * kr, axis=-1, keepdims=True) * (1.0 / 32.0)
    kr = _rope(kr * lax.rsqrt(ssk + EPS) * gk, cm, sm)
    kr = jnp.where(ropel, kr, 0.0)
    for h in range(8):
        slab = kn[:, h * LANES:(h + 1) * LANES]
        put(kc_ref, h, jnp.where(ones_at_96, 1.0, jnp.where(nope, rms64(slab) * gk, kr)))
        put(vc_ref, h, jnp.where(ones_at_64, 1.0, vm[:, h * LANES:(h + 1) * LANES]))

    for j, slab in enumerate(_head_slabs(proj("d_q"))):
        put(qd_ref, j, jnp.where(ones_at_64, shift_ref[2:3, :], nr64(slab, 4, 64 ** -0.5 * LOG2E)))
    for j, slab in enumerate(_head_slabs(proj("d_k"))):
        put(kd_ref, j, jnp.where(ones_at_64, 1.0, nr64(slab, 5, 1.0)))
    v = proj("d_v")
    ones_at_0 = jnp.where(lane == 0, 1.0, 0.0)
    for h in range(4):
        put(vd_ref, 2 * h, v[:, h * LANES:(h + 1) * LANES])
        put(vd_ref, 2 * h + 1, ones_at_0)


def _prep_call(xall, modsel, gain, wl, shift_tab, tabs, n_lat):
    r = xall.shape[0]
    nt = r // ROW_TILE
    row = lambda w: pl.BlockSpec((ROW_TILE, w), lambda i: (i, 0))
    out_w = dict(qa=1024, ka=256, va=256, qb=256, kb=256, vb=512, gb=512,
                 qc=1024, kc=1024, vc=1024, qd=1024, kd=1024, vd=1024)
    out_shape = [jax.ShapeDtypeStruct((r, w), F32 if n == "gb" else BF16) for n, w in out_w.items()]
    outs = pl.pallas_call(
        _prep_kernel,
        grid=(nt,),
        in_specs=[row(D_MODEL),
                  _mod_spec(n_lat // ROW_TILE),
                  _const_spec((1, D_MODEL)),
                  _const_spec((D_MODEL, MAIN_WIDTH)),
                  _const_spec((384, 1024)), _const_spec((256, 1024)), _const_spec((256, 1024)),
                  _const_spec((8, LANES)), _const_spec((8, LANES)), _const_spec((1, 384)), _const_spec((1, 256)),
                  row(LANES), row(LANES), row(LANES), row(LANES)],
        out_specs=[row(w) for w in out_w.values()],
        out_shape=out_shape,
        compiler_params=_params(("arbitrary",)),
        name="prep",
    )(xall, modsel, gain, wl["w_main"], wl["wuq"], wl["wuk"], wl["wuv"],
      wl["head_gain"], shift_tab, wl["cq_gain"], wl["ckv_gain"], *tabs)
    return dict(zip(out_w.keys(), outs))


N_PAD_CHUNKS = PAD // RET_CHUNK


def _ret_kernel(dec_ref, qf_ref, kf_ref, vf_ref, qr_ref, kr_ref, vr_ref,
                of_ref, or_ref, state_ref, intra_ref, qdec_ref, kdec_ref, cdec_ref):
    c = RET_CHUNK
    t = pl.program_id(0)

    @pl.when(t == 0)
    def _tables():
        x = dec_ref[...]
        lg = jnp.minimum(x, 0.0) - jnp.log(1.0 + jnp.exp(-jnp.abs(x)))
        i_idx = lax.broadcasted_iota(jnp.int32, (c, c), 0).astype(F32)
        j_idx = lax.broadcasted_iota(jnp.int32, (c, c), 1).astype(F32)
        for d in range(2):
            for h in range(N_HEADS_RET):
                g = lg[d:d + 1, h:h + 1]
                if d == 0:
                    rel, qe, ke = i_idx - j_idx, i_idx + 1.0, c - 1.0 - i_idx
                else:
                    rel, qe, ke = j_idx - i_idx, c - i_idx, i_idx
                intra_ref[d, h] = jnp.where(rel >= 0, jnp.exp(g * jnp.maximum(rel, 0.0)), 0.0)
                qdec_ref[d, h] = jnp.exp(g * qe)
                kdec_ref[d, h] = jnp.exp(g * ke[:, :64])
                cdec_ref[d, h] = jnp.exp(jnp.broadcast_to(g, (8, LANES)) * float(c))

    @pl.when((t == 0) | (t == N_PAD_CHUNKS))
    def _clear():
        state_ref[...] = jnp.zeros_like(state_ref)

    for d, (q_ref, k_ref, v_ref, o_ref) in enumerate(
            ((qf_ref, kf_ref, vf_ref, of_ref), (qr_ref, kr_ref, vr_ref, or_ref))):
        for h in range(N_HEADS_RET):
            qh = q_ref[:, h * 64:(h + 1) * 64]
            kh = k_ref[:, h * 64:(h + 1) * 64]
            vh = v_ref[:, h * LANES:(h + 1) * LANES]
            sc = lax.dot_general(qh, kh, (((1,), (1,)), ((), ())), preferred_element_type=F32)
            sc = sc * intra_ref[d, h]
            st = state_ref[d, h]
            o = _dot(sc.astype(BF16), vh) + _dot(qh, st.astype(BF16)) * qdec_ref[d, h]
            o_ref[:, h * LANES:(h + 1) * LANES] = o
            kd = (kh.astype(F32) * kdec_ref[d, h]).astype(BF16)
            upd = lax.dot_general(kd, vh, (((0,), (0,)), ((), ())), preferred_element_type=F32)
            state_ref[d, h] = st * cdec_ref[d, h][0:1, :] + upd


def _ret_call(qb, kb, vb, dec, n_lat):
    r = qb.shape[0]
    nc = r // RET_CHUNK
    nl = n_lat // RET_CHUNK
    nctx = CTX // RET_CHUNK
    npad = N_PAD_CHUNKS
    assert nc == nl + nctx + npad
    fwd = lambda t: (jnp.where(t < npad, nl + nctx + t,
                               jnp.where(t < npad + nctx, nl + t - npad, t - npad - nctx)), 0)
    bwd = lambda t: (jnp.where(t < npad, nl + nctx + t, nc - 1 - t), 0)
    blk = lambda w, m: pl.BlockSpec((RET_CHUNK, w), m)
    return pl.pallas_call(
        _ret_kernel,
        grid=(nc,),
        in_specs=[_const_spec((8, LANES)),
                  blk(256, fwd), blk(256, fwd), blk(512, fwd),
                  blk(256, bwd), blk(256, bwd), blk(512, bwd)],
        out_specs=[blk(512, fwd), blk(512, bwd)],
        out_shape=[jax.ShapeDtypeStruct((r, 512), F32)] * 2,
        scratch_shapes=[pltpu.VMEM((2, N_HEADS_RET, 64, LANES), F32),
                        pltpu.VMEM((2, N_HEADS_RET, RET_CHUNK, RET_CHUNK), F32),
                        pltpu.VMEM((2, N_HEADS_RET, RET_CHUNK, RET_CHUNK), F32),
                        pltpu.VMEM((2, N_HEADS_RET, RET_CHUNK, 64), F32),
                        pltpu.VMEM((2, N_HEADS_RET, 8, LANES), F32)],
        compiler_params=_params(("arbitrary",)),
        name="retention",
    )(dec, qb, kb, vb, qb, kb, vb)


def _attn_kernel(*refs, nm, dv, n_lat, diff_scale):
    if diff_scale is None:
        qt_ref, k_ref, vl_ref, vc_ref, o_ref, m_ref, acc_ref, s_ref, p_ref, mx_ref, alpha_ref = refs
    else:
        qt_ref, k_ref, vl_ref, vc_ref, lam_ref, sg_ref, o_ref, m_ref, acc_ref, s_ref, p_ref, mx_ref, alpha_ref = refs
    i = pl.program_id(1)
    n_chunks = n_lat // KV_CHUNK
    tq = qt_ref.shape[1]
    maps = range(nm)

    def scores(k, a):
        return _dot(k[:, a * LANES:(a + 1) * LANES], qt_ref[a * LANES:(a + 1) * LANES, :])

    kc = k_ref[n_lat:n_lat + CTX, :]
    for a in maps:
        s = scores(kc, a)
        m = jnp.max(s, axis=0, keepdims=True)
        m_ref[a] = m
        acc_ref[a] = _dot(vc_ref[0], jnp.exp2(s - m).astype(BF16))

    def qk(j, slot):
        k = k_ref[pl.ds(pl.multiple_of(j * KV_CHUNK, KV_CHUNK), KV_CHUNK), :]
        for a in maps:
            s = scores(k, a)
            s_ref[a, slot] = s
            mx_ref[a, slot] = jnp.max(s, axis=0, keepdims=True)

    @pl.when(i < n_lat // tq)
    def _latent_keys():
        qk(0, 0)
        p_ref[:, 1] = jnp.zeros((nm,) + p_ref.shape[2:], BF16)
        alpha_ref[...] = jnp.ones_like(alpha_ref)

        def stage(j, slot, prefetch=True, softmax=True):
            other = 1 - slot
            if isinstance(j, int):
                j_prev, k_start = max(j - 1, 0), (j + 1) * KV_CHUNK
            else:
                j_prev, k_start = jnp.maximum(j - 1, 0), pl.multiple_of((j + 1) * KV_CHUNK, KV_CHUNK)
            k_rows = pl.ds(k_start, KV_CHUNK)
            for c in range(tq // Q_BLOCK):
                cols = slice(c * Q_BLOCK, (c + 1) * Q_BLOCK)
                for a in maps:
                    heads = slice(a * LANES, (a + 1) * LANES)
                    if softmax:
                        m_old = m_ref[a, :, cols]
                        m_new = jnp.maximum(m_old, mx_ref[a, slot, :, cols])
                        p_ref[a, slot, :, cols] = jnp.exp2(s_ref[a, slot, :, cols] - m_new).astype(BF16)
                        m_ref[a, :, cols] = m_new
                    pv_c = _dot(vl_ref[0, j_prev], p_ref[a, other, :, cols])
                    if prefetch:
                        s = _dot(k_ref[k_rows, heads], qt_ref[heads, cols])
                        s_ref[a, other, :, cols] = s
                        mx_ref[a, other, :, cols] = jnp.max(s, axis=0, keepdims=True)
                    acc_ref[a, :, cols] = acc_ref[a, :, cols] * alpha_ref[a, :, cols] + pv_c
                    if softmax:
                        alpha_ref[a, :, cols] = jnp.exp2(m_old - m_new)

        def trip(t, carry):
            for u in range(STAGES_PER_TRIP):
                stage(STAGES_PER_TRIP * t + u, u % 2)
            return carry

        n_trips = n_chunks // STAGES_PER_TRIP - 1
        lax.fori_loop(0, n_trips, trip, 0)
        for j in range(n_trips * STAGES_PER_TRIP, n_chunks):
            stage(j, j % 2, prefetch=j + 1 < n_chunks)
        stage(n_chunks, n_chunks % 2, prefetch=False, softmax=False)

    _attn_finish(acc_ref, o_ref, None if diff_scale is None else (lam_ref, sg_ref), dv, diff_scale)


def _attn_finish(acc_ref, o_ref, diff_refs, dv, diff_scale):
    def normalised(a):
        acc = acc_ref[a]
        return acc[:dv] * (1.0 / acc[dv:dv + 1])

    if diff_scale is None:
        o = normalised(0)
        o = jnp.concatenate([o, jnp.zeros((LANES - dv, o.shape[1]), F32)], axis=0)
        o_ref[...] = o.T.astype(o_ref.dtype)
    else:
        lam_ref, sg_ref = diff_refs
        lv = lam_ref[...]
        lam_init = 1.0 - diff_scale
        lam = (jnp.exp(jnp.sum(lv[0:1] * lv[1:2], axis=-1, keepdims=True))
               - jnp.exp(jnp.sum(lv[2:3] * lv[3:4], axis=-1, keepdims=True)) + lam_init)
        o = (normalised(0) - lam * normalised(1)).T
        o_ref[...] = (_rms(o, float(dv)) * sg_ref[...] * diff_scale).astype(o_ref.dtype)


def _attn_fixed_shift_kernel(*refs, nm, dv, n_lat, diff_scale):
    if diff_scale is None:
        qt_ref, k_ref, vl_ref, vc_ref, o_ref, acc_ref, p_ref = refs
        diff_refs = None
    else:
        qt_ref, k_ref, vl_ref, vc_ref, lam_ref, sg_ref, o_ref, acc_ref, p_ref = refs
        diff_refs = (lam_ref, sg_ref)
    i = pl.program_id(1)
    n_chunks = n_lat // KV_CHUNK
    tq = qt_ref.shape[1]
    maps = range(nm)
    blocks = [(slice(c * Q_BLOCK, (c + 1) * Q_BLOCK), a, slice(a * LANES, (a + 1) * LANES))
              for c in range(tq // Q_BLOCK) for a in maps]

    def probs(k_rows, heads, cols):
        return jnp.exp2(_dot(k_ref[k_rows, heads], qt_ref[heads, cols])).astype(BF16)

    ctx_rows = slice(n_lat, n_lat + CTX)
    for cols, a, heads in blocks:
        acc_ref[a, :, cols] = _dot(vc_ref[0], probs(ctx_rows, heads, cols))

    @pl.when(i < n_lat // tq)
    def _latent_keys():
        for cols, a, heads in blocks:
            p_ref[a, 0, :, cols] = probs(slice(0, KV_CHUNK), heads, cols)

        def stage(j, slot, prefetch=True):
            if isinstance(j, int):
                k_start = (j + 1) * KV_CHUNK
            else:
                k_start = pl.multiple_of((j + 1) * KV_CHUNK, KV_CHUNK)
            for cols, a, heads in blocks:
                pv = _dot(vl_ref[0, j], p_ref[a, slot, :, cols])
                if prefetch:
                    p_ref[a, 1 - slot, :, cols] = probs(pl.ds(k_start, KV_CHUNK), heads, cols)
                acc_ref[a, :, cols] = acc_ref[a, :, cols] + pv

        def trip(t, carry):
            for u in range(STAGES_PER_TRIP):
                stage(STAGES_PER_TRIP * t + u, u % 2)
            return carry

        n_trips = n_chunks // STAGES_PER_TRIP - 1
        lax.fori_loop(0, n_trips, trip, 0)
        for j in range(n_trips * STAGES_PER_TRIP, n_chunks):
            stage(j, j % 2, prefetch=j + 1 < n_chunks)

    _attn_finish(acc_ref, o_ref, diff_refs, dv, diff_scale)


def _attn_call(q, k, v, shift_is_safe, *, n_heads, n_kv, nm, dv, n_lat, tq, name, extra=(), diff_scale=None):
    r = q.shape[0]
    assert n_lat % tq == 0 and r - n_lat == CTX <= tq
    nq = n_lat // tq + 1
    group = n_heads // n_kv
    dvp = -(-(dv + 1) // BF16_ROWS) * BF16_ROWS
    slab = v.shape[1] // n_kv
    n_chunks = n_lat // KV_CHUNK
    qt = jnp.pad(q.T, ((0, 0), (0, nq * tq - r)))
    v3 = v.reshape(r, n_kv, slab)[:, :, :dvp]
    vl = v3[:n_lat].reshape(n_chunks, KV_CHUNK, n_kv, dvp).transpose(2, 0, 3, 1)
    vc = v3[n_lat:].transpose(1, 2, 0)
    extra_specs = [pl.BlockSpec(e.shape, lambda h, i: (0, 0)) for e in extra]
    once = pl.Buffered(1)
    acc = pltpu.VMEM((nm, dvp, tq), F32)
    probs = pltpu.VMEM((nm, 2, KV_CHUNK, tq), BF16)

    def call(body, scratch, suffix):
        return pl.pallas_call(
            functools.partial(body, nm=nm, dv=dv, n_lat=n_lat, diff_scale=diff_scale),
            grid=(n_heads, nq),
            in_specs=[pl.BlockSpec((nm * LANES, tq), lambda h, i: (h, i)),
                      pl.BlockSpec((r, nm * LANES), lambda h, i: (0, h // group), pipeline_mode=once),
                      pl.BlockSpec((1, n_chunks, dvp, KV_CHUNK), lambda h, i: (h // group, 0, 0, 0),
                                   pipeline_mode=once),
                      pl.BlockSpec((1, dvp, CTX), lambda h, i: (h // group, 0, 0))] + extra_specs,
            out_specs=pl.BlockSpec((tq, LANES), lambda h, i: (i, h)),
            out_shape=jax.ShapeDtypeStruct((nq * tq, n_heads * LANES), BF16),
            scratch_shapes=scratch,
            compiler_params=_params(("arbitrary", "arbitrary")),
            name=name + suffix,
        )(qt, k, vl, vc, *extra)

    running_max = [pltpu.VMEM((nm, 1, tq), F32), acc,
                   pltpu.VMEM((nm, 2, KV_CHUNK, tq), F32),
                   probs,
                   pltpu.VMEM((nm, 2, 1, tq), F32),
                   pltpu.VMEM((nm, 1, tq), F32)]
    return lax.cond(shift_is_safe,
                    lambda: call(_attn_fixed_shift_kernel, [acc, probs], "_fixed"),
                    lambda: call(_attn_kernel, running_max, ""))


def _merge_kernel(x_ref, mod_ref, gain_ref, wg_ref, oa_ref, of_ref, or_ref, gb_ref, oc_ref, od_ref,
                  wa_ref, wb_ref, wc_ref, wd_ref, wo_ref, out_ref):
    x = x_ref[...]
    mod = mod_ref[0]
    hb = _prenorm(x, gain_ref[...], mod, 0).astype(BF16)
    ob = of_ref[...] + or_ref[...]
    g = gb_ref[...]
    parts = [_rms(ob[:, h * LANES:(h + 1) * LANES], float(LANES)) for h in range(N_HEADS_RET)]
    bb = (jnp.concatenate(parts, axis=-1) * _silu(g)).astype(BF16)
    branches = ((oa_ref[...], wa_ref), (bb, wb_ref), (oc_ref[...], wc_ref), (od_ref[...], wd_ref))
    acc = None
    for n, (b, w_ref) in enumerate(branches):
        gate = _sigmoid(_dot(hb, wg_ref[:, n * D_MODEL:(n + 1) * D_MODEL]))
        term = gate * _dot(b, w_ref[...])
        acc = term if acc is None else acc + term
    y = _dot(acc.astype(BF16), wo_ref[...])
    out_ref[...] = x + mod[2:3, :] * y


def _merge_call(xall, modsel, gain, wl, oa, of, orv, gb, oc, od, *, n_rows, n_lat):
    nt = n_rows // ROW_TILE
    row = lambda w: pl.BlockSpec((ROW_TILE, w), lambda i: (i, 0))
    return pl.pallas_call(
        _merge_kernel,
        grid=(nt,),
        in_specs=[row(D_MODEL),
                  _mod_spec(n_lat // ROW_TILE),
                  _const_spec((1, D_MODEL)),
                  _const_spec((D_MODEL, 4 * D_MODEL)),
                  row(1024), row(512), row(512), row(512), row(1024), row(512),
                  _const_spec((1024, D_MODEL)), _const_spec((512, D_MODEL)),
                  _const_spec((1024, D_MODEL)), _const_spec((512, D_MODEL)),
                  _const_spec((D_MODEL, D_MODEL))],
        out_specs=row(D_MODEL),
        out_shape=jax.ShapeDtypeStruct((n_rows, D_MODEL), F32),
        compiler_params=_params(("arbitrary",)),
        name="merge",
    )(xall, modsel, gain, wl["w_gate"], oa, of, orv, gb, oc, od,
      wl["wb_a"], wl["wb_b"], wl["wb_c"], wl["wb_d"], wl["w_out"])


def _ffn_kernel(x_ref, mod_ref, gain_ref, wi_ref, wo_ref, out_ref):
    x = x_ref[...]
    mod = mod_ref[0]
    hb = _prenorm(x, gain_ref[...], mod, 3).astype(BF16)
    gate = _dot(hb, wi_ref[:, :FFN_HIDDEN])
    up = _dot(hb, wi_ref[:, FFN_HIDDEN:])
    y = _dot((_silu(gate) * up).astype(BF16), wo_ref[...])
    out_ref[...] = x + mod[5:6, :] * y


def _ffn_call(x1, modsel, gain, wl, n_lat):
    r = x1.shape[0]
    nt = r // ROW_TILE
    row = pl.BlockSpec((ROW_TILE, D_MODEL), lambda i: (i, 0))
    return pl.pallas_call(
        _ffn_kernel,
        grid=(nt,),
        in_specs=[row, _mod_spec(n_lat // ROW_TILE),
                  _const_spec((1, D_MODEL)),
                  _const_spec((D_MODEL, 2 * FFN_HIDDEN)),
                  _const_spec((FFN_HIDDEN, D_MODEL))],
        out_specs=row,
        out_shape=jax.ShapeDtypeStruct((r, D_MODEL), F32),
        compiler_params=_params(("arbitrary",)),
        name="ffn",
    )(x1, modsel, gain, wl["w_ffn_in"], wl["w_ffn_out"])


def _pad_rows_per_head(w, n_heads, width):
    w = w.reshape(n_heads, width, w.shape[-1])
    w = jnp.pad(w, ((0, 0), (0, LANES - width), (0, 0)))
    return w.reshape(n_heads * LANES, -1)


def _pad_cols_per_head(w, n_heads, width):
    w = w.reshape(w.shape[0], n_heads, width)
    w = jnp.pad(w, ((0, 0), (0, 0), (0, LANES - width)))
    return w.reshape(w.shape[0], n_heads * LANES)


def _pad_vec(v):
    return jnp.pad(v, (0, LANES - v.shape[0]))


def _layer_weights(l, w_in, gqa_qk_gain, mla_cq_gain, mla_ckv_gain, mla_w_uq, mla_w_ukv,
                   mla_qk_gain, diff_qk_gain, w_branch, w_out, w_ffn_in, w_ffn_out):
    wi = w_in[l]
    w_main = jnp.concatenate([wi[:, :2976], jnp.zeros((D_MODEL, 96), F32), wi[:, 2976:4512]], axis=1)
    ukv = mla_w_ukv[l].reshape(256, 8, 128)
    head_gain = jnp.stack([
        _pad_vec(gqa_qk_gain[l, 0]), _pad_vec(gqa_qk_gain[l, 1]),
        _pad_vec(mla_qk_gain[l, 0]), _pad_vec(mla_qk_gain[l, 1]),
        _pad_vec(diff_qk_gain[l, 0]), _pad_vec(diff_qk_gain[l, 1]),
        jnp.zeros((LANES,), F32), jnp.zeros((LANES,), F32)])
    wb = w_branch[l]
    return dict(
        w_main=w_main.astype(BF16),
        w_gate=wi[:, 4512:].astype(BF16),
        wuq=_pad_cols_per_head(mla_w_uq[l], 8, 96).astype(BF16),
        wuk=_pad_cols_per_head(ukv[:, :, :64].reshape(256, 512), 8, 64).astype(BF16),
        wuv=_pad_cols_per_head(ukv[:, :, 64:].reshape(256, 512), 8, 64).astype(BF16),
        head_gain=head_gain,
        cq_gain=mla_cq_gain[l][None], ckv_gain=mla_ckv_gain[l][None],
        wb_a=_pad_rows_per_head(wb[0], 8, 64).astype(BF16),
        wb_b=wb[1].astype(BF16),
        wb_c=_pad_rows_per_head(wb[2], 8, 64).astype(BF16),
        wb_d=wb[3].astype(BF16),
        w_out=w_out[l].astype(BF16),
        w_ffn_in=w_ffn_in[l].astype(BF16),
        w_ffn_out=w_ffn_out[l].astype(BF16),
    )


def _score_bounds(gqa_gain, mla_gain, diff_gain):
    amax = lambda g: jnp.max(jnp.abs(g.astype(F32)))
    margin = 1.01
    b_gqa = 64 * amax(gqa_gain[0]) * amax(gqa_gain[1]) * 64 ** -0.5
    b_diff = 64 * amax(diff_gain[0]) * amax(diff_gain[1]) * 64 ** -0.5
    norm = lambda g: jnp.sqrt(64 * amax(g[:64]) ** 2 + 32 * amax(g[64:]) ** 2)
    b_mla = norm(mla_gain[0]) * norm(mla_gain[1]) * 96 ** -0.5
    return jnp.stack([b_gqa, b_mla, b_diff]) * (LOG2E * margin)


def _rope_tables(seq, n_rest):
    n_rows = seq // GRID_W
    row = jnp.repeat(jnp.arange(n_rows, dtype=F32), GRID_W)
    col = jnp.tile(jnp.arange(GRID_W, dtype=F32), n_rows)

    def pairs(dim):
        quarter = dim // 4
        freqs = ROPE_THETA ** (-jnp.arange(quarter, dtype=F32) / quarter)
        ang = jnp.concatenate([row[:, None] * freqs, col[:, None] * freqs], axis=-1)
        cos = jnp.repeat(jnp.cos(ang), 2, axis=-1)
        sin = jnp.repeat(jnp.sin(ang), 2, axis=-1) * jnp.tile(jnp.array([-1.0, 1.0], F32), dim // 2)
        return cos, sin

    c64, s64 = pairs(64)
    c32, s32 = pairs(32)
    one = lambda w: jnp.ones((seq, w), F32)
    zero = lambda w: jnp.zeros((seq, w), F32)
    t64c = jnp.concatenate([c64, c64], axis=-1)
    t64s = jnp.concatenate([s64, s64], axis=-1)
    tmc = jnp.concatenate([one(64), c32, one(32)], axis=-1)
    tms = jnp.concatenate([zero(64), s32, zero(32)], axis=-1)
    rest_c = jnp.ones((n_rest, LANES), F32)
    rest_s = jnp.zeros((n_rest, LANES), F32)
    cat = lambda t, rest: jnp.concatenate([t, rest], axis=0)
    return cat(t64c, rest_c), cat(t64s, rest_s), cat(tmc, rest_c), cat(tms, rest_s)


def kernel(x, c, ctx, c_ctx, w_mod, b_mod, norm_gain, w_in, gqa_qk_gain, ret_decay, mla_cq_gain,
           mla_ckv_gain, mla_w_uq, mla_w_ukv, mla_qk_gain, diff_qk_gain, diff_lambda, diff_subln_gain,
           w_branch, w_out, w_ffn_in, w_ffn_out):
    depth = w_mod.shape[0]
    batch, seq, _ = x.shape
    assert batch == 1 and ctx.shape[1] == CTX and seq % (STAGES_PER_TRIP * KV_CHUNK) == 0
    n_all = seq + CTX + PAD

    tabs = _rope_tables(seq, CTX + PAD)
    cc = jnp.concatenate([c_ctx[None], c, jnp.zeros((6, D_MODEL), F32)], axis=0)
    mods = _mod_call(cc, w_mod, b_mod)
    xall = jnp.concatenate([x[0], ctx[0], jnp.zeros((PAD, D_MODEL), F32)], axis=0)

    for l in range(depth):
        last = l == depth - 1
        lam_init = 0.8 - 0.6 * math.exp(-0.3 * l)
        wl = _layer_weights(l, w_in, gqa_qk_gain, mla_cq_gain, mla_ckv_gain, mla_w_uq, mla_w_ukv,
                            mla_qk_gain, diff_qk_gain, w_branch, w_out, w_ffn_in, w_ffn_out)
        modsel = jnp.pad(mods[l, :2].reshape(2, 6, D_MODEL), ((0, 0), (0, 2), (0, 0)))
        g_attn = norm_gain[l, 0][None]
        g_ffn = norm_gain[l, 1][None]

        bounds = _score_bounds(gqa_qk_gain[l], mla_qk_gain[l], diff_qk_gain[l])
        safe = bounds <= SHIFT_MAX
        shift_tab = jnp.pad(jnp.broadcast_to(-jnp.minimum(bounds, SHIFT_MAX)[:, None], (3, LANES)),
                            ((0, 5), (0, 0)))
        p = _prep_call(xall, modsel, g_attn, wl, shift_tab, tabs, seq)
        dec = jnp.pad(ret_decay[l].astype(F32), ((0, 6), (0, LANES - N_HEADS_RET)))
        of, orv = _ret_call(p["qb"], p["kb"], p["vb"], dec, seq)
        oa = _attn_call(p["qa"], p["ka"], p["va"], safe[0], n_heads=8, n_kv=2, nm=1, dv=64, n_lat=seq,
                        tq=ATT_TQ, name="attn_gqa")
        oc = _attn_call(p["qc"], p["kc"], p["vc"], safe[1], n_heads=8, n_kv=8, nm=1, dv=64, n_lat=seq,
                        tq=ATT_TQ, name="attn_mla")
        lam_tab = jnp.pad(diff_lambda[l].astype(F32), ((0, 4), (0, LANES - 64)))
        od = _attn_call(p["qd"], p["kd"], p["vd"], safe[2], n_heads=4, n_kv=4, nm=2, dv=128, n_lat=seq,
                        tq=ATT_TQ, name="attn_diff", extra=(lam_tab, diff_subln_gain[l][None]),
                        diff_scale=1.0 - lam_init)
        n_rows = seq if last else n_all
        x1 = _merge_call(xall, modsel, g_attn, wl, oa, of, orv, p["gb"], oc, od, n_rows=n_rows, n_lat=seq)
        xall = _ffn_call(x1, modsel, g_ffn, wl, seq)

    return xall[None]
```

```python
import functools
import math

import jax
import jax.numpy as jnp
from jax import lax
from jax.experimental import pallas as pl
from jax.experimental.pallas import tpu as pltpu

F32 = jnp.float32
BF16 = jnp.bfloat16

D_MODEL = 1024
GRID_W = 64
ROPE_THETA = 10000.0
EPS = 1e-6
RET_CHUNK = 128
N_HEADS_RET = 4
FFN_HIDDEN = 2816
LANES = 128
BF16_ROWS = 16
ROW_TILE = 256
CTX = 256
PAD = 0
ATT_TQ = 2048
Q_BLOCK = 256
KV_CHUNK = 512
STAGES_PER_TRIP = 2
LOG2E = 1.4426950408889634
SHIFT_MAX = 40.0
VMEM_LIMIT = 56 * 1024 * 1024

SEG = dict(a_q=(0, 512), a_k=(512, 640), a_v=(640, 768),
           b_q=(768, 1024), b_k=(1024, 1280), b_v=(1280, 1792), b_g=(1792, 2304),
           c_q=(2304, 2688), c_kv=(2688, 2944), c_kr=(2944, 3072),
           d_q=(3072, 3584), d_k=(3584, 4096), d_v=(4096, 4608))
MAIN_WIDTH = 4608


def _params(sem):
    return pltpu.CompilerParams(dimension_semantics=sem, vmem_limit_bytes=VMEM_LIMIT)


def _const_spec(shape):
    n = len(shape)
    return pl.BlockSpec(shape, lambda *_: (0,) * n, pipeline_mode=pl.Buffered(1))


def _mod_spec(n_lat_tiles):
    return pl.BlockSpec((1, 8, D_MODEL), lambda i: (jnp.where(i < n_lat_tiles, 1, 0), 0, 0))


def _sigmoid(x):
    return 1.0 / (1.0 + jnp.exp(-x))


def _silu(x):
    return x * _sigmoid(x)


def _rms(x, n):
    ss = jnp.sum(x * x, axis=-1, keepdims=True) * (1.0 / n)
    return x * lax.rsqrt(ss + EPS)


def _lane(shape):
    return lax.broadcasted_iota(jnp.int32, shape, 1)


def _rope(y, cos, sin_signed):
    even = (_lane(y.shape) & 1) == 0
    partner = jnp.where(even, pltpu.roll(y, LANES - 1, 1), pltpu.roll(y, 1, 1))
    return y * cos + partner * sin_signed


def _head_slabs(p):
    lo = _lane((p.shape[0], LANES)) < 64
    out = []
    for j in range(p.shape[1] // LANES):
        src = p[:, j * LANES:(j + 1) * LANES]
        out.append(jnp.where(lo, src, 0.0))
        out.append(jnp.where(lo, pltpu.roll(src, 64, 1), 0.0))
    return out


def _dot(a, b):
    return jnp.dot(a, b, preferred_element_type=F32)


def _mod_kernel(c_ref, w_ref, b_ref, o_ref):
    s = _silu(c_ref[...]).astype(BF16)
    o_ref[0] = _dot(s, w_ref[0].astype(BF16)) + b_ref[0]


def _mod_call(cc, w_mod, b_mod):
    depth = w_mod.shape[0]
    nt = 6
    return pl.pallas_call(
        _mod_kernel,
        grid=(depth, nt),
        in_specs=[pl.BlockSpec((8, D_MODEL), lambda l, j: (0, 0)),
                  pl.BlockSpec((1, D_MODEL, D_MODEL), lambda l, j: (l, 0, j)),
                  pl.BlockSpec((1, 1, D_MODEL), lambda l, j: (l, 0, j))],
        out_specs=pl.BlockSpec((1, 8, D_MODEL), lambda l, j: (l, 0, j)),
        out_shape=jax.ShapeDtypeStruct((depth, 8, 6 * D_MODEL), F32),
        compiler_params=_params(("arbitrary", "arbitrary")),
        name="mod",
    )(cc, w_mod, b_mod.reshape(depth, 1, 6 * D_MODEL))


def _prenorm(x, gain, mod, shift_row):
    h = _rms(x, D_MODEL) * gain
    return h * (1.0 + mod[shift_row + 1:shift_row + 2, :]) + mod[shift_row:shift_row + 1, :]


def _prep_kernel(x_ref, mod_ref, gain_ref, w_ref, wuq_ref, wuk_ref, wuv_ref,
                 hg_ref, shift_ref, cqg_ref, ckvg_ref, c64_ref, s64_ref, cm_ref, sm_ref,
                 qa_ref, ka_ref, va_ref, qb_ref, kb_ref, vb_ref, gb_ref,
                 qc_ref, kc_ref, vc_ref, qd_ref, kd_ref, vd_ref):
    hb = _prenorm(x_ref[...], gain_ref[...], mod_ref[0], 0).astype(BF16)
    t = hb.shape[0]
    lane = _lane((t, LANES))
    c64, s64 = c64_ref[...], s64_ref[...]
    cm, sm = cm_ref[...], sm_ref[...]

    def proj(name):
        a, b = SEG[name]
        return _dot(hb, w_ref[:, a:b])

    def put(ref, j, val):
        ref[:, j * LANES:(j + 1) * LANES] = val.astype(ref.dtype)

    def rms64(slab):
        ss = jnp.sum(slab * slab, axis=-1, keepdims=True) * (1.0 / 64.0)
        return slab * lax.rsqrt(ss + EPS)

    def nr64(slab, gain_row, scale):
        y = rms64(slab) * hg_ref[gain_row:gain_row + 1, :]
        y = _rope(y, c64, s64)
        return y * scale if scale != 1.0 else y

    ones_at_64 = lane == 64
    ones_at_96 = lane == 96

    for j, slab in enumerate(_head_slabs(proj("a_q"))):
        put(qa_ref, j, jnp.where(ones_at_64, shift_ref[0:1, :], nr64(slab, 0, 64 ** -0.5 * LOG2E)))
    for j, slab in enumerate(_head_slabs(proj("a_k"))):
        put(ka_ref, j, jnp.where(ones_at_64, 1.0, nr64(slab, 1, 1.0)))
    for j, slab in enumerate(_head_slabs(proj("a_v"))):
        put(va_ref, j, jnp.where(ones_at_64, 1.0, slab))

    q = proj("b_q")
    k = proj("b_k") * (64 ** -0.5)
    for j in range(2):
        put(qb_ref, j, _rope(q[:, j * LANES:(j + 1) * LANES], c64, s64))
        put(kb_ref, j, _rope(k[:, j * LANES:(j + 1) * LANES], c64, s64))
    vb_ref[...] = proj("b_v").astype(vb_ref.dtype)
    gb_ref[...] = proj("b_g")

    nope = lane < 64
    ropel = (lane >= 64) & (lane < 96)
    cq = (_rms(proj("c_q"), 384.0) * cqg_ref[...]).astype(BF16)
    qm = _dot(cq, wuq_ref[...])
    gq = hg_ref[2:3, :]
    for h in range(8):
        slab = qm[:, h * LANES:(h + 1) * LANES]
        sq = slab * slab
        ssn = jnp.sum(jnp.where(nope, sq, 0.0), axis=-1, keepdims=True) * (1.0 / 64.0)
        ssr = jnp.sum(jnp.where(ropel, sq, 0.0), axis=-1, keepdims=True) * (1.0 / 32.0)
        inv = jnp.where(nope, lax.rsqrt(ssn + EPS), lax.rsqrt(ssr + EPS))
        y = _rope(slab * inv * gq, cm, sm) * (96 ** -0.5 * LOG2E)
        put(qc_ref, h, jnp.where(ones_at_96, shift_ref[1:2, :], y))
    ckv = (_rms(proj("c_kv"), 256.0) * ckvg_ref[...]).astype(BF16)
    kn = _dot(ckv, wuk_ref[...])
    vm = _dot(ckv, wuv_ref[...])
    gk = hg_ref[3:4, :]
    kr = pltpu.roll(proj("c_kr"), 64, 1)
    ssk = jnp.sum(kr * kr, axis=-1, keepdims=True) * (1.0 / 32.0)
    kr = _rope(kr * lax.rsqrt(ssk + EPS) * gk, cm, sm)
    kr = jnp.where(ropel, kr, 0.0)
    for h in range(8):
        slab = kn[:, h * LANES:(h + 1) * LANES]
        put(kc_ref, h, jnp.where(ones_at_96, 1.0, jnp.where(nope, rms64(slab) * gk, kr)))
        put(vc_ref, h, jnp.where(ones_at_64, 1.0, vm[:, h * LANES:(h + 1) * LANES]))

    for j, slab in enumerate(_head_slabs(proj("d_q"))):
        put(qd_ref, j, jnp.where(ones_at_64, shift_ref[2:3, :], nr64(slab, 4, 64 ** -0.5 * LOG2E)))
    for j, slab in enumerate(_head_slabs(proj("d_k"))):
        put(kd_ref, j, jnp.where(ones_at_64, 1.0, nr64(slab, 5, 1.0)))
    v = proj("d_v")
    ones_at_0 = jnp.where(lane == 0, 1.0, 0.0)
    for h in range(4):
        put(vd_ref, 2 * h, v[:, h * LANES:(h + 1) * LANES])
        put(vd_ref, 2 * h + 1, ones_at_0)


def _prep_call(xall, modsel, gain, wl, shift_tab, tabs, n_lat):
    r = xall.shape[0]
    nt = r // ROW_TILE
    row = lambda w: pl.BlockSpec((ROW_TILE, w), lambda i: (i, 0))
    out_w = dict(qa=1024, ka=256, va=256, qb=256, kb=256, vb=512, gb=512,
                 qc=1024, kc=1024, vc=1024, qd=1024, kd=1024, vd=1024)
    out_shape = [jax.ShapeDtypeStruct((r, w), F32 if n == "gb" else BF16) for n, w in out_w.items()]
    outs = pl.pallas_call(
        _prep_kernel,
        grid=(nt,),
        in_specs=[row(D_MODEL),
                  _mod_spec(n_lat // ROW_TILE),
                  _const_spec((1, D_MODEL)),
                  _const_spec((D_MODEL, MAIN_WIDTH)),
                  _const_spec((384, 1024)), _const_spec((256, 1024)), _const_spec((256, 1024)),
                  _const_spec((8, LANES)), _const_spec((8, LANES)), _const_spec((1, 384)), _const_spec((1, 256)),
                  row(LANES), row(LANES), row(LANES), row(LANES)],
        out_specs=[row(w) for w in out_w.values()],
        out_shape=out_shape,
        compiler_params=_params(("arbitrary",)),
        name="prep",
    )(xall, modsel, gain, wl["w_main"], wl["wuq"], wl["wuk"], wl["wuv"],
      wl["head_gain"], shift_tab, wl["cq_gain"], wl["ckv_gain"], *tabs)
    return dict(zip(out_w.keys(), outs))


N_PAD_CHUNKS = PAD // RET_CHUNK


def _ret_kernel(dec_ref, qf_ref, kf_ref, vf_ref, qr_ref, kr_ref, vr_ref,
                of_ref, or_ref, state_ref, intra_ref, qdec_ref, kdec_ref, cdec_ref):
    c = RET_CHUNK
    t = pl.program_id(0)

    @pl.when(t == 0)
    def _tables():
        x = dec_ref[...]
        lg = jnp.minimum(x, 0.0) - jnp.log(1.0 + jnp.exp(-jnp.abs(x)))
        i_idx = lax.broadcasted_iota(jnp.int32, (c, c), 0).astype(F32)
        j_idx = lax.broadcasted_iota(jnp.int32, (c, c), 1).astype(F32)
        for d in range(2):
            for h in range(N_HEADS_RET):
                g = lg[d:d + 1, h:h + 1]
                if d == 0:
                    rel, qe, ke = i_idx - j_idx, i_idx + 1.0, c - 1.0 - i_idx
                else:
                    rel, qe, ke = j_idx - i_idx, c - i_idx, i_idx
                intra_ref[d, h] = jnp.where(rel >= 0, jnp.exp(g * jnp.maximum(rel, 0.0)), 0.0)
                qdec_ref[d, h] = jnp.exp(g * qe)
                kdec_ref[d, h] = jnp.exp(g * ke[:, :64])
                cdec_ref[d, h] = jnp.exp(jnp.broadcast_to(g, (8, LANES)) * float(c))

    @pl.when((t == 0) | (t == N_PAD_CHUNKS))
    def _clear():
        state_ref[...] = jnp.zeros_like(state_ref)

    for d, (q_ref, k_ref, v_ref, o_ref) in enumerate(
            ((qf_ref, kf_ref, vf_ref, of_ref), (qr_ref, kr_ref, vr_ref, or_ref))):
        for h in range(N_HEADS_RET):
            qh = q_ref[:, h * 64:(h + 1) * 64]
            kh = k_ref[:, h * 64:(h + 1) * 64]
            vh = v_ref[:, h * LANES:(h + 1) * LANES]
            sc = lax.dot_general(qh, kh, (((1,), (1,)), ((), ())), preferred_element_type=F32)
            sc = sc * intra_ref[d, h]
            st = state_ref[d, h]
            o = _dot(sc.astype(BF16), vh) + _dot(qh, st.astype(BF16)) * qdec_ref[d, h]
            o_ref[:, h * LANES:(h + 1) * LANES] = o
            kd = (kh.astype(F32) * kdec_ref[d, h]).astype(BF16)
            upd = lax.dot_general(kd, vh, (((0,), (0,)), ((), ())), preferred_element_type=F32)
            state_ref[d, h] = st * cdec_ref[d, h][0:1, :] + upd


def _ret_call(qb, kb, vb, dec, n_lat):
    r = qb.shape[0]
    nc = r // RET_CHUNK
    nl = n_lat // RET_CHUNK
    nctx = CTX // RET_CHUNK
    npad = N_PAD_CHUNKS
    assert nc == nl + nctx + npad
    fwd = lambda t: (jnp.where(t < npad, nl + nctx + t,
                               jnp.where(t < npad + nctx, nl + t - npad, t - npad - nctx)), 0)
    bwd = lambda t: (jnp.where(t < npad, nl + nctx + t, nc - 1 - t), 0)
    blk = lambda w, m: pl.BlockSpec((RET_CHUNK, w), m)
    return pl.pallas_call(
        _ret_kernel,
        grid=(nc,),
        in_specs=[_const_spec((8, LANES)),
                  blk(256, fwd), blk(256, fwd), blk(512, fwd),
                  blk(256, bwd), blk(256, bwd), blk(512, bwd)],
        out_specs=[blk(512, fwd), blk(512, bwd)],
        out_shape=[jax.ShapeDtypeStruct((r, 512), F32)] * 2,
        scratch_shapes=[pltpu.VMEM((2, N_HEADS_RET, 64, LANES), F32),
                        pltpu.VMEM((2, N_HEADS_RET, RET_CHUNK, RET_CHUNK), F32),
                        pltpu.VMEM((2, N_HEADS_RET, RET_CHUNK, RET_CHUNK), F32),
                        pltpu.VMEM((2, N_HEADS_RET, RET_CHUNK, 64), F32),
                        pltpu.VMEM((2, N_HEADS_RET, 8, LANES), F32)],
        compiler_params=_params(("arbitrary",)),
        name="retention",
    )(dec, qb, kb, vb, qb, kb, vb)


def _attn_kernel(*refs, nm, dv, n_lat, diff_scale):
    if diff_scale is None:
        qt_ref, k_ref, vl_ref, vc_ref, o_ref, m_ref, acc_ref, s_ref, p_ref, mx_ref, alpha_ref = refs
    else:
        qt_ref, k_ref, vl_ref, vc_ref, lam_ref, sg_ref, o_ref, m_ref, acc_ref, s_ref, p_ref, mx_ref, alpha_ref = refs
    i = pl.program_id(1)
    n_chunks = n_lat // KV_CHUNK
    tq = qt_ref.shape[1]
    maps = range(nm)

    def scores(k, a):
        return _dot(k[:, a * LANES:(a + 1) * LANES], qt_ref[a * LANES:(a + 1) * LANES, :])

    kc = k_ref[n_lat:n_lat + CTX, :]
    for a in maps:
        s = scores(kc, a)
        m = jnp.max(s, axis=0, keepdims=True)
        m_ref[a] = m
        acc_ref[a] = _dot(vc_ref[0], jnp.exp2(s - m).astype(BF16))

    def qk(j, slot):
        k = k_ref[pl.ds(pl.multiple_of(j * KV_CHUNK, KV_CHUNK), KV_CHUNK), :]
        for a in maps:
            s = scores(k, a)
            s_ref[a, slot] = s
            mx_ref[a, slot] = jnp.max(s, axis=0, keepdims=True)

    @pl.when(i < n_lat // tq)
    def _latent_keys():
        qk(0, 0)
        p_ref[:, 1] = jnp.zeros((nm,) + p_ref.shape[2:], BF16)
        alpha_ref[...] = jnp.ones_like(alpha_ref)

        def stage(j, slot, prefetch=True, softmax=True):
            other = 1 - slot
            if isinstance(j, int):
                j_prev, k_start = max(j - 1, 0), (j + 1) * KV_CHUNK
            else:
                j_prev, k_start = jnp.maximum(j - 1, 0), pl.multiple_of((j + 1) * KV_CHUNK, KV_CHUNK)
            k_rows = pl.ds(k_start, KV_CHUNK)
            for c in range(tq // Q_BLOCK):
                cols = slice(c * Q_BLOCK, (c + 1) * Q_BLOCK)
                for a in maps:
                    heads = slice(a * LANES, (a + 1) * LANES)
                    if softmax:
                        m_old = m_ref[a, :, cols]
                        m_new = jnp.maximum(m_old, mx_ref[a, slot, :, cols])
                        p_ref[a, slot, :, cols] = jnp.exp2(s_ref[a, slot, :, cols] - m_new).astype(BF16)
                        m_ref[a, :, cols] = m_new
                    pv_c = _dot(vl_ref[0, j_prev], p_ref[a, other, :, cols])
                    if prefetch:
                        s = _dot(k_ref[k_rows, heads], qt_ref[heads, cols])
                        s_ref[a, other, :, cols] = s
                        mx_ref[a, other, :, cols] = jnp.max(s, axis=0, keepdims=True)
                    acc_ref[a, :, cols] = acc_ref[a, :, cols] * alpha_ref[a, :, cols] + pv_c
                    if softmax:
                        alpha_ref[a, :, cols] = jnp.exp2(m_old - m_new)

        def trip(t, carry):
            for u in range(STAGES_PER_TRIP):
                stage(STAGES_PER_TRIP * t + u, u % 2)
            return carry

        n_trips = n_chunks // STAGES_PER_TRIP - 1
        lax.fori_loop(0, n_trips, trip, 0)
        for j in range(n_trips * STAGES_PER_TRIP, n_chunks):
            stage(j, j % 2, prefetch=j + 1 < n_chunks)
        stage(n_chunks, n_chunks % 2, prefetch=False, softmax=False)

    _attn_finish(acc_ref, o_ref, None if diff_scale is None else (lam_ref, sg_ref), dv, diff_scale)


def _attn_finish(acc_ref, o_ref, diff_refs, dv, diff_scale):
    def normalised(a):
        acc = acc_ref[a]
        return acc[:dv] * (1.0 / acc[dv:dv + 1])

    if diff_scale is None:
        o = normalised(0)
        o = jnp.concatenate([o, jnp.zeros((LANES - dv, o.shape[1]), F32)], axis=0)
        o_ref[...] = o.T.astype(o_ref.dtype)
    else:
        lam_ref, sg_ref = diff_refs
        lv = lam_ref[...]
        lam_init = 1.0 - diff_scale
        lam = (jnp.exp(jnp.sum(lv[0:1] * lv[1:2], axis=-1, keepdims=True))
               - jnp.exp(jnp.sum(lv[2:3] * lv[3:4], axis=-1, keepdims=True)) + lam_init)
        o = (normalised(0) - lam * normalised(1)).T
        o_ref[...] = (_rms(o, float(dv)) * sg_ref[...] * diff_scale).astype(o_ref.dtype)


def _attn_fixed_shift_kernel(*refs, nm, dv, n_lat, diff_scale):
    if diff_scale is None:
        qt_ref, k_ref, vl_ref, vc_ref, o_ref, acc_ref, p_ref = refs
        diff_refs = None
    else:
        qt_ref, k_ref, vl_ref, vc_ref, lam_ref, sg_ref, o_ref, acc_ref, p_ref = refs
        diff_refs = (lam_ref, sg_ref)
    i = pl.program_id(1)
    n_chunks = n_lat // KV_CHUNK
    tq = qt_ref.shape[1]
    maps = range(nm)
    blocks = [(slice(c * Q_BLOCK, (c + 1) * Q_BLOCK), a, slice(a * LANES, (a + 1) * LANES))
              for c in range(tq // Q_BLOCK) for a in maps]

    def probs(k_rows, heads, cols):
        return jnp.exp2(_dot(k_ref[k_rows, heads], qt_ref[heads, cols])).astype(BF16)

    def context_keys(cols, a, heads):
        acc_ref[a, :, cols] = _dot(vc_ref[0], probs(slice(n_lat, n_lat + CTX), heads, cols))

    @pl.when(i >= n_lat // tq)
    def _context_queries():
        for block in blocks:
            context_keys(*block)

    @pl.when(i < n_lat // tq)
    def _latent_queries():
        for cols, a, heads in blocks:
            context_keys(cols, a, heads)
            p_ref[a, 0, :, cols] = probs(slice(0, KV_CHUNK), heads, cols)

        def stage(j, slot, prefetch=True):
            if isinstance(j, int):
                k_start = (j + 1) * KV_CHUNK
            else:
                k_start = pl.multiple_of((j + 1) * KV_CHUNK, KV_CHUNK)
            for cols, a, heads in blocks:
                pv = _dot(vl_ref[0, j], p_ref[a, slot, :, cols])
                if prefetch:
                    p_ref[a, 1 - slot, :, cols] = probs(pl.ds(k_start, KV_CHUNK), heads, cols)
                acc_ref[a, :, cols] = acc_ref[a, :, cols] + pv

        def trip(t, carry):
            for u in range(STAGES_PER_TRIP):
                stage(STAGES_PER_TRIP * t + u, u % 2)
            return carry

        n_trips = n_chunks // STAGES_PER_TRIP - 1
        lax.fori_loop(0, n_trips, trip, 0)
        for j in range(n_trips * STAGES_PER_TRIP, n_chunks):
            stage(j, j % 2, prefetch=j + 1 < n_chunks)

    _attn_finish(acc_ref, o_ref, diff_refs, dv, diff_scale)


def _attn_dispatch_kernel(flag_ref, *refs, nm, dv, n_lat, diff_scale):
    n_in = 4 if diff_scale is None else 6
    ins, o_ref = refs[:n_in], refs[n_in]
    m_ref, acc_ref, s_ref, p_ref, mx_ref, alpha_ref = refs[n_in + 1:]
    static = dict(nm=nm, dv=dv, n_lat=n_lat, diff_scale=diff_scale)

    @pl.when(flag_ref[0] != 0)
    def _fixed_shift():
        _attn_fixed_shift_kernel(*ins, o_ref, acc_ref, p_ref, **static)

    @pl.when(flag_ref[0] == 0)
    def _running_max():
        _attn_kernel(*ins, o_ref, m_ref, acc_ref, s_ref, p_ref, mx_ref, alpha_ref, **static)


def _attn_call(q, k, v, shift_is_safe, *, n_heads, n_kv, nm, dv, n_lat, tq, name, extra=(), diff_scale=None):
    r = q.shape[0]
    assert n_lat % tq == 0 and r - n_lat == CTX <= tq
    nq = n_lat // tq + 1
    group = n_heads // n_kv
    dvp = -(-(dv + 1) // BF16_ROWS) * BF16_ROWS
    slab = v.shape[1] // n_kv
    n_chunks = n_lat // KV_CHUNK
    qt = jnp.pad(q.T, ((0, 0), (0, nq * tq - r)))
    v3 = v.reshape(r, n_kv, slab)[:, :, :dvp]
    vl = v3[:n_lat].reshape(n_chunks, KV_CHUNK, n_kv, dvp).transpose(2, 0, 3, 1)
    vc = v3[n_lat:].transpose(1, 2, 0)
    extra_specs = [pl.BlockSpec(e.shape, lambda h, i, f: (0, 0)) for e in extra]
    once = pl.Buffered(1)
    grid_spec = pltpu.PrefetchScalarGridSpec(
        num_scalar_prefetch=1,
        grid=(n_heads, nq),
        in_specs=[pl.BlockSpec((nm * LANES, tq), lambda h, i, f: (h, i)),
                  pl.BlockSpec((r, nm * LANES), lambda h, i, f: (0, h // group), pipeline_mode=once),
                  pl.BlockSpec((1, n_chunks, dvp, KV_CHUNK), lambda h, i, f: (h // group, 0, 0, 0),
                               pipeline_mode=once),
                  pl.BlockSpec((1, dvp, CTX), lambda h, i, f: (h // group, 0, 0))] + extra_specs,
        out_specs=pl.BlockSpec((tq, LANES), lambda h, i, f: (i, h)),
        scratch_shapes=[pltpu.VMEM((nm, 1, tq), F32),
                        pltpu.VMEM((nm, dvp, tq), F32),
                        pltpu.VMEM((nm, 2, KV_CHUNK, tq), F32),
                        pltpu.VMEM((nm, 2, KV_CHUNK, tq), BF16),
                        pltpu.VMEM((nm, 2, 1, tq), F32),
                        pltpu.VMEM((nm, 1, tq), F32)])
    return pl.pallas_call(
        functools.partial(_attn_dispatch_kernel, nm=nm, dv=dv, n_lat=n_lat, diff_scale=diff_scale),
        grid_spec=grid_spec,
        out_shape=jax.ShapeDtypeStruct((nq * tq, n_heads * LANES), BF16),
        compiler_params=_params(("arbitrary", "arbitrary")),
        name=name,
    )(shift_is_safe.astype(jnp.int32).reshape(1), qt, k, vl, vc, *extra)


def _merge_kernel(x_ref, mod_ref, gain_ref, wg_ref, oa_ref, of_ref, or_ref, gb_ref, oc_ref, od_ref,
                  wa_ref, wb_ref, wc_ref, wd_ref, wo_ref, out_ref):
    x = x_ref[...]
    mod = mod_ref[0]
    hb = _prenorm(x, gain_ref[...], mod, 0).astype(BF16)
    ob = of_ref[...] + or_ref[...]
    g = gb_ref[...]
    parts = [_rms(ob[:, h * LANES:(h + 1) * LANES], float(LANES)) for h in range(N_HEADS_RET)]
    bb = (jnp.concatenate(parts, axis=-1) * _silu(g)).astype(BF16)
    branches = ((oa_ref[...], wa_ref), (bb, wb_ref), (oc_ref[...], wc_ref), (od_ref[...], wd_ref))
    acc = None
    for n, (b, w_ref) in enumerate(branches):
        gate = _sigmoid(_dot(hb, wg_ref[:, n * D_MODEL:(n + 1) * D_MODEL]))
        term = gate * _dot(b, w_ref[...])
        acc = term if acc is None else acc + term
    y = _dot(acc.astype(BF16), wo_ref[...])
    out_ref[...] = x + mod[2:3, :] * y


def _merge_call(xall, modsel, gain, wl, oa, of, orv, gb, oc, od, *, n_rows, n_lat):
    nt = n_rows // ROW_TILE
    row = lambda w: pl.BlockSpec((ROW_TILE, w), lambda i: (i, 0))
    return pl.pallas_call(
        _merge_kernel,
        grid=(nt,),
        in_specs=[row(D_MODEL),
                  _mod_spec(n_lat // ROW_TILE),
                  _const_spec((1, D_MODEL)),
                  _const_spec((D_MODEL, 4 * D_MODEL)),
                  row(1024), row(512), row(512), row(512), row(1024), row(512),
                  _const_spec((1024, D_MODEL)), _const_spec((512, D_MODEL)),
                  _const_spec((1024, D_MODEL)), _const_spec((512, D_MODEL)),
                  _const_spec((D_MODEL, D_MODEL))],
        out_specs=row(D_MODEL),
        out_shape=jax.ShapeDtypeStruct((n_rows, D_MODEL), F32),
        compiler_params=_params(("arbitrary",)),
        name="merge",
    )(xall, modsel, gain, wl["w_gate"], oa, of, orv, gb, oc, od,
      wl["wb_a"], wl["wb_b"], wl["wb_c"], wl["wb_d"], wl["w_out"])


def _ffn_kernel(x_ref, mod_ref, gain_ref, wi_ref, wo_ref, out_ref):
    x = x_ref[...]
    mod = mod_ref[0]
    hb = _prenorm(x, gain_ref[...], mod, 3).astype(BF16)
    gate = _dot(hb, wi_ref[:, :FFN_HIDDEN])
    up = _dot(hb, wi_ref[:, FFN_HIDDEN:])
    y = _dot((_silu(gate) * up).astype(BF16), wo_ref[...])
    out_ref[...] = x + mod[5:6, :] * y


def _ffn_call(x1, modsel, gain, wl, n_lat):
    r = x1.shape[0]
    nt = r // ROW_TILE
    row = pl.BlockSpec((ROW_TILE, D_MODEL), lambda i: (i, 0))
    return pl.pallas_call(
        _ffn_kernel,
        grid=(nt,),
        in_specs=[row, _mod_spec(n_lat // ROW_TILE),
                  _const_spec((1, D_MODEL)),
                  _const_spec((D_MODEL, 2 * FFN_HIDDEN)),
                  _const_spec((FFN_HIDDEN, D_MODEL))],
        out_specs=row,
        out_shape=jax.ShapeDtypeStruct((r, D_MODEL), F32),
        compiler_params=_params(("arbitrary",)),
        name="ffn",
    )(x1, modsel, gain, wl["w_ffn_in"], wl["w_ffn_out"])


def _pad_rows_per_head(w, n_heads, width):
    w = w.reshape(n_heads, width, w.shape[-1])
    w = jnp.pad(w, ((0, 0), (0, LANES - width), (0, 0)))
    return w.reshape(n_heads * LANES, -1)


def _pad_cols_per_head(w, n_heads, width):
    w = w.reshape(w.shape[0], n_heads, width)
    w = jnp.pad(w, ((0, 0), (0, 0), (0, LANES - width)))
    return w.reshape(w.shape[0], n_heads * LANES)


def _pad_vec(v):
    return jnp.pad(v, (0, LANES - v.shape[0]))


def _layer_weights(l, w_in, gqa_qk_gain, mla_cq_gain, mla_ckv_gain, mla_w_uq, mla_w_ukv,
                   mla_qk_gain, diff_qk_gain, w_branch, w_out, w_ffn_in, w_ffn_out):
    wi = w_in[l]
    w_main = jnp.concatenate([wi[:, :2976], jnp.zeros((D_MODEL, 96), F32), wi[:, 2976:4512]], axis=1)
    ukv = mla_w_ukv[l].reshape(256, 8, 128)
    head_gain = jnp.stack([
        _pad_vec(gqa_qk_gain[l, 0]), _pad_vec(gqa_qk_gain[l, 1]),
        _pad_vec(mla_qk_gain[l, 0]), _pad_vec(mla_qk_gain[l, 1]),
        _pad_vec(diff_qk_gain[l, 0]), _pad_vec(diff_qk_gain[l, 1]),
        jnp.zeros((LANES,), F32), jnp.zeros((LANES,), F32)])
    wb = w_branch[l]
    return dict(
        w_main=w_main.astype(BF16),
        w_gate=wi[:, 4512:].astype(BF16),
        wuq=_pad_cols_per_head(mla_w_uq[l], 8, 96).astype(BF16),
        wuk=_pad_cols_per_head(ukv[:, :, :64].reshape(256, 512), 8, 64).astype(BF16),
        wuv=_pad_cols_per_head(ukv[:, :, 64:].reshape(256, 512), 8, 64).astype(BF16),
        head_gain=head_gain,
        cq_gain=mla_cq_gain[l][None], ckv_gain=mla_ckv_gain[l][None],
        wb_a=_pad_rows_per_head(wb[0], 8, 64).astype(BF16),
        wb_b=wb[1].astype(BF16),
        wb_c=_pad_rows_per_head(wb[2], 8, 64).astype(BF16),
        wb_d=wb[3].astype(BF16),
        w_out=w_out[l].astype(BF16),
        w_ffn_in=w_ffn_in[l].astype(BF16),
        w_ffn_out=w_ffn_out[l].astype(BF16),
    )


def _score_bounds(gqa_gain, mla_gain, diff_gain):
    amax = lambda g: jnp.max(jnp.abs(g.astype(F32)))
    margin = 1.01
    b_gqa = 64 * amax(gqa_gain[0]) * amax(gqa_gain[1]) * 64 ** -0.5
    b_diff = 64 * amax(diff_gain[0]) * amax(diff_gain[1]) * 64 ** -0.5
    norm = lambda g: jnp.sqrt(64 * amax(g[:64]) ** 2 + 32 * amax(g[64:]) ** 2)
    b_mla = norm(mla_gain[0]) * norm(mla_gain[1]) * 96 ** -0.5
    return jnp.stack([b_gqa, b_mla, b_diff]) * (LOG2E * margin)


def _rope_tables(seq, n_rest):
    n_rows = seq // GRID_W
    row = jnp.repeat(jnp.arange(n_rows, dtype=F32), GRID_W)
    col = jnp.tile(jnp.arange(GRID_W, dtype=F32), n_rows)

    def pairs(dim):
        quarter = dim // 4
        freqs = ROPE_THETA ** (-jnp.arange(quarter, dtype=F32) / quarter)
        ang = jnp.concatenate([row[:, None] * freqs, col[:, None] * freqs], axis=-1)
        cos = jnp.repeat(jnp.cos(ang), 2, axis=-1)
        sin = jnp.repeat(jnp.sin(ang), 2, axis=-1) * jnp.tile(jnp.array([-1.0, 1.0], F32), dim // 2)
        return cos, sin

    c64, s64 = pairs(64)
    c32, s32 = pairs(32)
    one = lambda w: jnp.ones((seq, w), F32)
    zero = lambda w: jnp.zeros((seq, w), F32)
    t64c = jnp.concatenate([c64, c64], axis=-1)
    t64s = jnp.concatenate([s64, s64], axis=-1)
    tmc = jnp.concatenate([one(64), c32, one(32)], axis=-1)
    tms = jnp.concatenate([zero(64), s32, zero(32)], axis=-1)
    rest_c = jnp.ones((n_rest, LANES), F32)
    rest_s = jnp.zeros((n_rest, LANES), F32)
    cat = lambda t, rest: jnp.concatenate([t, rest], axis=0)
    return cat(t64c, rest_c), cat(t64s, rest_s), cat(tmc, rest_c), cat(tms, rest_s)


def kernel(x, c, ctx, c_ctx, w_mod, b_mod, norm_gain, w_in, gqa_qk_gain, ret_decay, mla_cq_gain,
           mla_ckv_gain, mla_w_uq, mla_w_ukv, mla_qk_gain, diff_qk_gain, diff_lambda, diff_subln_gain,
           w_branch, w_out, w_ffn_in, w_ffn_out):
    depth = w_mod.shape[0]
    batch, seq, _ = x.shape
    assert batch == 1 and ctx.shape[1] == CTX and seq % (STAGES_PER_TRIP * KV_CHUNK) == 0
    n_all = seq + CTX + PAD

    tabs = _rope_tables(seq, CTX + PAD)
    cc = jnp.concatenate([c_ctx[None], c, jnp.zeros((6, D_MODEL), F32)], axis=0)
    mods = _mod_call(cc, w_mod, b_mod)
    xall = jnp.concatenate([x[0], ctx[0], jnp.zeros((PAD, D_MODEL), F32)], axis=0)

    for l in range(depth):
        last = l == depth - 1
        lam_init = 0.8 - 0.6 * math.exp(-0.3 * l)
        wl = _layer_weights(l, w_in, gqa_qk_gain, mla_cq_gain, mla_ckv_gain, mla_w_uq, mla_w_ukv,
                            mla_qk_gain, diff_qk_gain, w_branch, w_out, w_ffn_in, w_ffn_out)
        modsel = jnp.pad(mods[l, :2].reshape(2, 6, D_MODEL), ((0, 0), (0, 2), (0, 0)))
        g_attn = norm_gain[l, 0][None]
        g_ffn = norm_gain[l, 1][None]

        bounds = _score_bounds(gqa_qk_gain[l], mla_qk_gain[l], diff_qk_gain[l])
        safe = bounds <= SHIFT_MAX
        shift_tab = jnp.pad(jnp.broadcast_to(-jnp.minimum(bounds, SHIFT_MAX)[:, None], (3, LANES)),
                            ((0, 5), (0, 0)))
        p = _prep_call(xall, modsel, g_attn, wl, shift_tab, tabs, seq)
        dec = jnp.pad(ret_decay[l].astype(F32), ((0, 6), (0, LANES - N_HEADS_RET)))
        of, orv = _ret_call(p["qb"], p["kb"], p["vb"], dec, seq)
        oa = _attn_call(p["qa"], p["ka"], p["va"], safe[0], n_heads=8, n_kv=2, nm=1, dv=64, n_lat=seq,
                        tq=ATT_TQ, name="attn_gqa")
        oc = _attn_call(p["qc"], p["kc"], p["vc"], safe[1], n_heads=8, n_kv=8, nm=1, dv=64, n_lat=seq,
                        tq=ATT_TQ, name="attn_mla")
        lam_tab = jnp.pad(diff_lambda[l].astype(F32), ((0, 4), (0, LANES - 64)))
        od = _attn_call(p["qd"], p["kd"], p["vd"], safe[2], n_heads=4, n_kv=4, nm=2, dv=128, n_lat=seq,
                        tq=ATT_TQ, name="attn_diff", extra=(lam_tab, diff_subln_gain[l][None]),
                        diff_scale=1.0 - lam_init)
        n_rows = seq if last else n_all
        x1 = _merge_call(xall, modsel, g_attn, wl, oa, of, orv, p["gb"], oc, od, n_rows=n_rows, n_lat=seq)
        xall = _ffn_call(x1, modsel, g_ffn, wl, seq)

    return xall[None]
```

```python
import functools
import math

import jax
import jax.numpy as jnp
from jax import lax
from jax.experimental import pallas as pl
from jax.experimental.pallas import tpu as pltpu

F32 = jnp.float32
BF16 = jnp.bfloat16

D_MODEL = 1024
GRID_W = 64
ROPE_THETA = 10000.0
EPS = 1e-6
RET_CHUNK = 128
N_HEADS_RET = 4
FFN_HIDDEN = 2816
LANES = 128
BF16_ROWS = 16
ROW_TILE = 256
CTX = 256
PAD = 0
ATT_TQ = 2048
Q_BLOCK = 256
KV_CHUNK = 512
STAGES_PER_TRIP = 2
FIXED_SHIFT_STAGES_PER_TRIP = 4
LOG2E = 1.4426950408889634
SHIFT_MAX = 40.0
VMEM_LIMIT = 56 * 1024 * 1024

SEG = dict(a_q=(0, 512), a_k=(512, 640), a_v=(640, 768),
           b_q=(768, 1024), b_k=(1024, 1280), b_v=(1280, 1792), b_g=(1792, 2304),
           c_q=(2304, 2688), c_kv=(2688, 2944), c_kr=(2944, 3072),
           d_q=(3072, 3584), d_k=(3584, 4096), d_v=(4096, 4608))
MAIN_WIDTH = 4608


def _params(sem):
    return pltpu.CompilerParams(dimension_semantics=sem, vmem_limit_bytes=VMEM_LIMIT)


def _const_spec(shape):
    n = len(shape)
    return pl.BlockSpec(shape, lambda *_: (0,) * n, pipeline_mode=pl.Buffered(1))


def _mod_spec(n_lat_tiles):
    return pl.BlockSpec((1, 8, D_MODEL), lambda i: (jnp.where(i < n_lat_tiles, 1, 0), 0, 0))


def _sigmoid(x):
    return 1.0 / (1.0 + jnp.exp(-x))


def _silu(x):
    return x * _sigmoid(x)


def _rms(x, n):
    ss = jnp.sum(x * x, axis=-1, keepdims=True) * (1.0 / n)
    return x * lax.rsqrt(ss + EPS)


def _lane(shape):
    return lax.broadcasted_iota(jnp.int32, shape, 1)


def _rope(y, cos, sin_signed):
    even = (_lane(y.shape) & 1) == 0
    partner = jnp.where(even, pltpu.roll(y, LANES - 1, 1), pltpu.roll(y, 1, 1))
    return y * cos + partner * sin_signed


def _head_slabs(p):
    lo = _lane((p.shape[0], LANES)) < 64
    out = []
    for j in range(p.shape[1] // LANES):
        src = p[:, j * LANES:(j + 1) * LANES]
        out.append(jnp.where(lo, src, 0.0))
        out.append(jnp.where(lo, pltpu.roll(src, 64, 1), 0.0))
    return out


def _dot(a, b):
    return jnp.dot(a, b, preferred_element_type=F32)


def _mod_kernel(c_ref, w_ref, b_ref, o_ref):
    s = _silu(c_ref[...]).astype(BF16)
    o_ref[0] = _dot(s, w_ref[0].astype(BF16)) + b_ref[0]


def _mod_call(cc, w_mod, b_mod):
    depth = w_mod.shape[0]
    nt = 6
    return pl.pallas_call(
        _mod_kernel,
        grid=(depth, nt),
        in_specs=[pl.BlockSpec((8, D_MODEL), lambda l, j: (0, 0)),
                  pl.BlockSpec((1, D_MODEL, D_MODEL), lambda l, j: (l, 0, j)),
                  pl.BlockSpec((1, 1, D_MODEL), lambda l, j: (l, 0, j))],
        out_specs=pl.BlockSpec((1, 8, D_MODEL), lambda l, j: (l, 0, j)),
        out_shape=jax.ShapeDtypeStruct((depth, 8, 6 * D_MODEL), F32),
        compiler_params=_params(("arbitrary", "arbitrary")),
        name="mod",
    )(cc, w_mod, b_mod.reshape(depth, 1, 6 * D_MODEL))


def _prenorm(x, gain, mod, shift_row):
    h = _rms(x, D_MODEL) * gain
    return h * (1.0 + mod[shift_row + 1:shift_row + 2, :]) + mod[shift_row:shift_row + 1, :]


def _prep_kernel(x_ref, mod_ref, gain_ref, w_ref, wuq_ref, wuk_ref, wuv_ref,
                 hg_ref, shift_ref, cqg_ref, ckvg_ref, c64_ref, s64_ref, cm_ref, sm_ref,
                 qa_ref, ka_ref, va_ref, qb_ref, kb_ref, vb_ref, gb_ref,
                 qc_ref, kc_ref, vc_ref, qd_ref, kd_ref, vd_ref):
    hb = _prenorm(x_ref[...], gain_ref[...], mod_ref[0], 0).astype(BF16)
    t = hb.shape[0]
    lane = _lane((t, LANES))
    c64, s64 = c64_ref[...], s64_ref[...]
    cm, sm = cm_ref[...], sm_ref[...]

    def proj(name):
        a, b = SEG[name]
        return _dot(hb, w_ref[:, a:b])

    def put(ref, j, val):
        ref[:, j * LANES:(j + 1) * LANES] = val.astype(ref.dtype)

    def rms64(slab):
        ss = jnp.sum(slab * slab, axis=-1, keepdims=True) * (1.0 / 64.0)
        return slab * lax.rsqrt(ss + EPS)

    def nr64(slab, gain_row, scale):
        y = rms64(slab) * hg_ref[gain_row:gain_row + 1, :]
        y = _rope(y, c64, s64)
        return y * scale if scale != 1.0 else y

    ones_at_64 = lane == 64
    ones_at_96 = lane == 96

    for j, slab in enumerate(_head_slabs(proj("a_q"))):
        put(qa_ref, j, jnp.where(ones_at_64, shift_ref[0:1, :], nr64(slab, 0, 64 ** -0.5 * LOG2E)))
    for j, slab in enumerate(_head_slabs(proj("a_k"))):
        put(ka_ref, j, jnp.where(ones_at_64, 1.0, nr64(slab, 1, 1.0)))
    for j, slab in enumerate(_head_slabs(proj("a_v"))):
        put(va_ref, j, jnp.where(ones_at_64, 1.0, slab))

    q = proj("b_q")
    k = proj("b_k") * (64 ** -0.5)
    for j in range(2):
        put(qb_ref, j, _rope(q[:, j * LANES:(j + 1) * LANES], c64, s64))
        put(kb_ref, j, _rope(k[:, j * LANES:(j + 1) * LANES], c64, s64))
    vb_ref[...] = proj("b_v").astype(vb_ref.dtype)
    gb_ref[...] = proj("b_g")

    nope = lane < 64
    ropel = (lane >= 64) & (lane < 96)
    cq = (_rms(proj("c_q"), 384.0) * cqg_ref[...]).astype(BF16)
    qm = _dot(cq, wuq_ref[...])
    gq = hg_ref[2:3, :]
    for h in range(8):
        slab = qm[:, h * LANES:(h + 1) * LANES]
        sq = slab * slab
        ssn = jnp.sum(jnp.where(nope, sq, 0.0), axis=-1, keepdims=True) * (1.0 / 64.0)
        ssr = jnp.sum(jnp.where(ropel, sq, 0.0), axis=-1, keepdims=True) * (1.0 / 32.0)
        inv = jnp.where(nope, lax.rsqrt(ssn + EPS), lax.rsqrt(ssr + EPS))
        y = _rope(slab * inv * gq, cm, sm) * (96 ** -0.5 * LOG2E)
        put(qc_ref, h, jnp.where(ones_at_96, shift_ref[1:2, :], y))
    ckv = (_rms(proj("c_kv"), 256.0) * ckvg_ref[...]).astype(BF16)
    kn = _dot(ckv, wuk_ref[...])
    vm = _dot(ckv, wuv_ref[...])
    gk = hg_ref[3:4, :]
    kr = pltpu.roll(proj("c_kr"), 64, 1)
    ssk = jnp.sum(kr * kr, axis=-1, keepdims=True) * (1.0 / 32.0)
    kr = _rope(kr * lax.rsqrt(ssk + EPS) * gk, cm, sm)
    kr = jnp.where(ropel, kr, 0.0)
    for h in range(8):
        slab = kn[:, h * LANES:(h + 1) * LANES]
        put(kc_ref, h, jnp.where(ones_at_96, 1.0, jnp.where(nope, rms64(slab) * gk, kr)))
        put(vc_ref, h, jnp.where(ones_at_64, 1.0, vm[:, h * LANES:(h + 1) * LANES]))

    for j, slab in enumerate(_head_slabs(proj("d_q"))):
        put(qd_ref, j, jnp.where(ones_at_64, shift_ref[2:3, :], nr64(slab, 4, 64 ** -0.5 * LOG2E)))
    for j, slab in enumerate(_head_slabs(proj("d_k"))):
        put(kd_ref, j, jnp.where(ones_at_64, 1.0, nr64(slab, 5, 1.0)))
    v = proj("d_v")
    ones_at_0 = jnp.where(lane == 0, 1.0, 0.0)
    for h in range(4):
        put(vd_ref, 2 * h, v[:, h * LANES:(h + 1) * LANES])
        put(vd_ref, 2 * h + 1, ones_at_0)


def _prep_call(xall, modsel, gain, wl, shift_tab, tabs, n_lat):
    r = xall.shape[0]
    nt = r // ROW_TILE
    row = lambda w: pl.BlockSpec((ROW_TILE, w), lambda i: (i, 0))
    out_w = dict(qa=1024, ka=256, va=256, qb=256, kb=256, vb=512, gb=512,
                 qc=1024, kc=1024, vc=1024, qd=1024, kd=1024, vd=1024)
    out_shape = [jax.ShapeDtypeStruct((r, w), F32 if n == "gb" else BF16) for n, w in out_w.items()]
    outs = pl.pallas_call(
        _prep_kernel,
        grid=(nt,),
        in_specs=[row(D_MODEL),
                  _mod_spec(n_lat // ROW_TILE),
                  _const_spec((1, D_MODEL)),
                  _const_spec((D_MODEL, MAIN_WIDTH)),
                  _const_spec((384, 1024)), _const_spec((256, 1024)), _const_spec((256, 1024)),
                  _const_spec((8, LANES)), _const_spec((8, LANES)), _const_spec((1, 384)), _const_spec((1, 256)),
                  row(LANES), row(LANES), row(LANES), row(LANES)],
        out_specs=[row(w) for w in out_w.values()],
        out_shape=out_shape,
        compiler_params=_params(("arbitrary",)),
        name="prep",
    )(xall, modsel, gain, wl["w_main"], wl["wuq"], wl["wuk"], wl["wuv"],
      wl["head_gain"], shift_tab, wl["cq_gain"], wl["ckv_gain"], *tabs)
    return dict(zip(out_w.keys(), outs))


N_PAD_CHUNKS = PAD // RET_CHUNK


def _ret_kernel(dec_ref, qf_ref, kf_ref, vf_ref, qr_ref, kr_ref, vr_ref,
                of_ref, or_ref, state_ref, intra_ref, qdec_ref, kdec_ref, cdec_ref):
    c = RET_CHUNK
    t = pl.program_id(0)

    @pl.when(t == 0)
    def _tables():
        x = dec_ref[...]
        lg = jnp.minimum(x, 0.0) - jnp.log(1.0 + jnp.exp(-jnp.abs(x)))
        i_idx = lax.broadcasted_iota(jnp.int32, (c, c), 0).astype(F32)
        j_idx = lax.broadcasted_iota(jnp.int32, (c, c), 1).astype(F32)
        for d in range(2):
            for h in range(N_HEADS_RET):
                g = lg[d:d + 1, h:h + 1]
                if d == 0:
                    rel, qe, ke = i_idx - j_idx, i_idx + 1.0, c - 1.0 - i_idx
                else:
                    rel, qe, ke = j_idx - i_idx, c - i_idx, i_idx
                intra_ref[d, h] = jnp.where(rel >= 0, jnp.exp(g * jnp.maximum(rel, 0.0)), 0.0)
                qdec_ref[d, h] = jnp.exp(g * qe)
                kdec_ref[d, h] = jnp.exp(g * ke[:, :64])
                cdec_ref[d, h] = jnp.exp(jnp.broadcast_to(g, (8, LANES)) * float(c))

    @pl.when((t == 0) | (t == N_PAD_CHUNKS))
    def _clear():
        state_ref[...] = jnp.zeros_like(state_ref)

    for d, (q_ref, k_ref, v_ref, o_ref) in enumerate(
            ((qf_ref, kf_ref, vf_ref, of_ref), (qr_ref, kr_ref, vr_ref, or_ref))):
        for h in range(N_HEADS_RET):
            qh = q_ref[:, h * 64:(h + 1) * 64]
            kh = k_ref[:, h * 64:(h + 1) * 64]
            vh = v_ref[:, h * LANES:(h + 1) * LANES]
            sc = lax.dot_general(qh, kh, (((1,), (1,)), ((), ())), preferred_element_type=F32)
            sc = sc * intra_ref[d, h]
            st = state_ref[d, h]
            o = _dot(sc.astype(BF16), vh) + _dot(qh, st.astype(BF16)) * qdec_ref[d, h]
            o_ref[:, h * LANES:(h + 1) * LANES] = o
            kd = (kh.astype(F32) * kdec_ref[d, h]).astype(BF16)
            upd = lax.dot_general(kd, vh, (((0,), (0,)), ((), ())), preferred_element_type=F32)
            state_ref[d, h] = st * cdec_ref[d, h][0:1, :] + upd


def _ret_call(qb, kb, vb, dec, n_lat):
    r = qb.shape[0]
    nc = r // RET_CHUNK
    nl = n_lat // RET_CHUNK
    nctx = CTX // RET_CHUNK
    npad = N_PAD_CHUNKS
    assert nc == nl + nctx + npad
    fwd = lambda t: (jnp.where(t < npad, nl + nctx + t,
                               jnp.where(t < npad + nctx, nl + t - npad, t - npad - nctx)), 0)
    bwd = lambda t: (jnp.where(t < npad, nl + nctx + t, nc - 1 - t), 0)
    blk = lambda w, m: pl.BlockSpec((RET_CHUNK, w), m)
    return pl.pallas_call(
        _ret_kernel,
        grid=(nc,),
        in_specs=[_const_spec((8, LANES)),
                  blk(256, fwd), blk(256, fwd), blk(512, fwd),
                  blk(256, bwd), blk(256, bwd), blk(512, bwd)],
        out_specs=[blk(512, fwd), blk(512, bwd)],
        out_shape=[jax.ShapeDtypeStruct((r, 512), F32)] * 2,
        scratch_shapes=[pltpu.VMEM((2, N_HEADS_RET, 64, LANES), F32),
                        pltpu.VMEM((2, N_HEADS_RET, RET_CHUNK, RET_CHUNK), F32),
                        pltpu.VMEM((2, N_HEADS_RET, RET_CHUNK, RET_CHUNK), F32),
                        pltpu.VMEM((2, N_HEADS_RET, RET_CHUNK, 64), F32),
                        pltpu.VMEM((2, N_HEADS_RET, 8, LANES), F32)],
        compiler_params=_params(("arbitrary",)),
        name="retention",
    )(dec, qb, kb, vb, qb, kb, vb)


def _attn_kernel(*refs, nm, dv, n_lat, diff_scale):
    if diff_scale is None:
        qt_ref, k_ref, vl_ref, vc_ref, o_ref, m_ref, acc_ref, s_ref, p_ref, mx_ref, alpha_ref = refs
    else:
        qt_ref, k_ref, vl_ref, vc_ref, lam_ref, sg_ref, o_ref, m_ref, acc_ref, s_ref, p_ref, mx_ref, alpha_ref = refs
    i = pl.program_id(1)
    n_chunks = n_lat // KV_CHUNK
    tq = qt_ref.shape[1]
    maps = range(nm)

    def scores(k, a):
        return _dot(k[:, a * LANES:(a + 1) * LANES], qt_ref[a * LANES:(a + 1) * LANES, :])

    kc = k_ref[n_lat:n_lat + CTX, :]
    for a in maps:
        s = scores(kc, a)
        m = jnp.max(s, axis=0, keepdims=True)
        m_ref[a] = m
        acc_ref[a] = _dot(vc_ref[0], jnp.exp2(s - m).astype(BF16))

    def qk(j, slot):
        k = k_ref[pl.ds(pl.multiple_of(j * KV_CHUNK, KV_CHUNK), KV_CHUNK), :]
        for a in maps:
            s = scores(k, a)
            s_ref[a, slot] = s
            mx_ref[a, slot] = jnp.max(s, axis=0, keepdims=True)

    @pl.when(i < n_lat // tq)
    def _latent_keys():
        qk(0, 0)
        p_ref[:, 1] = jnp.zeros((nm,) + p_ref.shape[2:], BF16)
        alpha_ref[...] = jnp.ones_like(alpha_ref)

        def stage(j, slot, prefetch=True, softmax=True):
            other = 1 - slot
            if isinstance(j, int):
                j_prev, k_start = max(j - 1, 0), (j + 1) * KV_CHUNK
            else:
                j_prev, k_start = jnp.maximum(j - 1, 0), pl.multiple_of((j + 1) * KV_CHUNK, KV_CHUNK)
            k_rows = pl.ds(k_start, KV_CHUNK)
            for c in range(tq // Q_BLOCK):
                cols = slice(c * Q_BLOCK, (c + 1) * Q_BLOCK)
                for a in maps:
                    heads = slice(a * LANES, (a + 1) * LANES)
                    if softmax:
                        m_old = m_ref[a, :, cols]
                        m_new = jnp.maximum(m_old, mx_ref[a, slot, :, cols])
                        p_ref[a, slot, :, cols] = jnp.exp2(s_ref[a, slot, :, cols] - m_new).astype(BF16)
                        m_ref[a, :, cols] = m_new
                    pv_c = _dot(vl_ref[0, j_prev], p_ref[a, other, :, cols])
                    if prefetch:
                        s = _dot(k_ref[k_rows, heads], qt_ref[heads, cols])
                        s_ref[a, other, :, cols] = s
                        mx_ref[a, other, :, cols] = jnp.max(s, axis=0, keepdims=True)
                    acc_ref[a, :, cols] = acc_ref[a, :, cols] * alpha_ref[a, :, cols] + pv_c
                    if softmax:
                        alpha_ref[a, :, cols] = jnp.exp2(m_old - m_new)

        def trip(t, carry):
            for u in range(STAGES_PER_TRIP):
                stage(STAGES_PER_TRIP * t + u, u % 2)
            return carry

        n_trips = n_chunks // STAGES_PER_TRIP - 1
        lax.fori_loop(0, n_trips, trip, 0)
        for j in range(n_trips * STAGES_PER_TRIP, n_chunks):
            stage(j, j % 2, prefetch=j + 1 < n_chunks)
        stage(n_chunks, n_chunks % 2, prefetch=False, softmax=False)

    _attn_finish(acc_ref, o_ref, None if diff_scale is None else (lam_ref, sg_ref), dv, diff_scale)


def _attn_finish(acc_ref, o_ref, diff_refs, dv, diff_scale):
    def normalised(a):
        acc = acc_ref[a]
        return acc[:dv] * (1.0 / acc[dv:dv + 1])

    if diff_scale is None:
        o = normalised(0)
        o = jnp.concatenate([o, jnp.zeros((LANES - dv, o.shape[1]), F32)], axis=0)
        o_ref[...] = o.T.astype(o_ref.dtype)
    else:
        lam_ref, sg_ref = diff_refs
        lv = lam_ref[...]
        lam_init = 1.0 - diff_scale
        lam = (jnp.exp(jnp.sum(lv[0:1] * lv[1:2], axis=-1, keepdims=True))
               - jnp.exp(jnp.sum(lv[2:3] * lv[3:4], axis=-1, keepdims=True)) + lam_init)
        o = (normalised(0) - lam * normalised(1)).T
        o_ref[...] = (_rms(o, float(dv)) * sg_ref[...] * diff_scale).astype(o_ref.dtype)


def _attn_fixed_shift_kernel(*refs, nm, dv, n_lat, diff_scale):
    if diff_scale is None:
        qt_ref, k_ref, vl_ref, vc_ref, o_ref, acc_ref, p_ref, l_ref = refs
        diff_refs = None
    else:
        qt_ref, k_ref, vl_ref, vc_ref, lam_ref, sg_ref, o_ref, acc_ref, p_ref, l_ref = refs
        diff_refs = (lam_ref, sg_ref)
    i = pl.program_id(1)
    n_chunks = n_lat // KV_CHUNK
    tq = qt_ref.shape[1]
    maps = range(nm)
    blocks = [(slice(c * Q_BLOCK, (c + 1) * Q_BLOCK), a, slice(a * LANES, (a + 1) * LANES))
              for c in range(tq // Q_BLOCK) for a in maps]

    def probs(k_rows, heads, cols):
        p = jnp.exp2(_dot(k_ref[k_rows, heads], qt_ref[heads, cols]))
        return p.astype(BF16), jnp.sum(p.reshape(p.shape[0] // 8, 8, p.shape[1]), axis=0)

    def context_keys(cols, a, heads):
        p, l8 = probs(slice(n_lat, n_lat + CTX), heads, cols)
        acc_ref[a, :dv, cols] = _dot(vc_ref[0, :dv, :], p)
        l_ref[a, :, cols] = l8

    def store_row_sums():
        for a in maps:
            acc_ref[a, dv:dv + 1, :] = jnp.sum(l_ref[a], axis=0, keepdims=True)

    @pl.when(i >= n_lat // tq)
    def _context_queries():
        for block in blocks:
            context_keys(*block)
        store_row_sums()

    @pl.when(i < n_lat // tq)
    def _latent_queries():
        for cols, a, heads in blocks:
            context_keys(cols, a, heads)
            p_ref[a, 0, :, cols], l8 = probs(slice(0, KV_CHUNK), heads, cols)
            l_ref[a, :, cols] = l_ref[a, :, cols] + l8

        def stage(j, slot, prefetch=True):
            if isinstance(j, int):
                k_start = (j + 1) * KV_CHUNK
            else:
                k_start = pl.multiple_of((j + 1) * KV_CHUNK, KV_CHUNK)
            for cols, a, heads in blocks:
                pv = _dot(vl_ref[0, j, :dv, :], p_ref[a, slot, :, cols])
                if prefetch:
                    p_ref[a, 1 - slot, :, cols], l8 = probs(pl.ds(k_start, KV_CHUNK), heads, cols)
                    l_ref[a, :, cols] = l_ref[a, :, cols] + l8
                acc_ref[a, :dv, cols] = acc_ref[a, :dv, cols] + pv

        def trip(t, carry):
            for u in range(FIXED_SHIFT_STAGES_PER_TRIP):
                stage(FIXED_SHIFT_STAGES_PER_TRIP * t + u, u % 2)
            return carry

        n_trips = n_chunks // FIXED_SHIFT_STAGES_PER_TRIP - 1
        lax.fori_loop(0, n_trips, trip, 0)
        for j in range(n_trips * FIXED_SHIFT_STAGES_PER_TRIP, n_chunks):
            stage(j, j % 2, prefetch=j + 1 < n_chunks)
        store_row_sums()

    _attn_finish(acc_ref, o_ref, diff_refs, dv, diff_scale)


def _attn_dispatch_kernel(flag_ref, *refs, nm, dv, n_lat, diff_scale):
    n_in = 4 if diff_scale is None else 6
    ins, o_ref = refs[:n_in], refs[n_in]
    m_ref, acc_ref, s_ref, p_ref, mx_ref, alpha_ref, l_ref = refs[n_in + 1:]
    static = dict(nm=nm, dv=dv, n_lat=n_lat, diff_scale=diff_scale)

    @pl.when(flag_ref[0] != 0)
    def _fixed_shift():
        _attn_fixed_shift_kernel(*ins, o_ref, acc_ref, p_ref, l_ref, **static)

    @pl.when(flag_ref[0] == 0)
    def _running_max():
        _attn_kernel(*ins, o_ref, m_ref, acc_ref, s_ref, p_ref, mx_ref, alpha_ref, **static)


def _attn_call(q, k, v, shift_is_safe, *, n_heads, n_kv, nm, dv, n_lat, tq, name, extra=(), diff_scale=None):
    r = q.shape[0]
    assert n_lat % tq == 0 and r - n_lat == CTX <= tq
    nq = n_lat // tq + 1
    group = n_heads // n_kv
    dvp = -(-(dv + 1) // BF16_ROWS) * BF16_ROWS
    slab = v.shape[1] // n_kv
    n_chunks = n_lat // KV_CHUNK
    qt = jnp.pad(q.T, ((0, 0), (0, nq * tq - r)))
    v3 = v.reshape(r, n_kv, slab)[:, :, :dvp]
    vl = v3[:n_lat].reshape(n_chunks, KV_CHUNK, n_kv, dvp).transpose(2, 0, 3, 1)
    vc = v3[n_lat:].transpose(1, 2, 0)
    extra_specs = [pl.BlockSpec(e.shape, lambda h, i, f: (0, 0)) for e in extra]
    once = pl.Buffered(1)
    grid_spec = pltpu.PrefetchScalarGridSpec(
        num_scalar_prefetch=1,
        grid=(n_heads, nq),
        in_specs=[pl.BlockSpec((nm * LANES, tq), lambda h, i, f: (h, i)),
                  pl.BlockSpec((r, nm * LANES), lambda h, i, f: (0, h // group), pipeline_mode=once),
                  pl.BlockSpec((1, n_chunks, dvp, KV_CHUNK), lambda h, i, f: (h // group, 0, 0, 0),
                               pipeline_mode=once),
                  pl.BlockSpec((1, dvp, CTX), lambda h, i, f: (h // group, 0, 0))] + extra_specs,
        out_specs=pl.BlockSpec((tq, LANES), lambda h, i, f: (i, h)),
        scratch_shapes=[pltpu.VMEM((nm, 1, tq), F32),
                        pltpu.VMEM((nm, dvp, tq), F32),
                        pltpu.VMEM((nm, 2, KV_CHUNK, tq), F32),
                        pltpu.VMEM((nm, 2, KV_CHUNK, tq), BF16),
                        pltpu.VMEM((nm, 2, 1, tq), F32),
                        pltpu.VMEM((nm, 1, tq), F32),
                        pltpu.VMEM((nm, 8, tq), F32)])
    return pl.pallas_call(
        functools.partial(_attn_dispatch_kernel, nm=nm, dv=dv, n_lat=n_lat, diff_scale=diff_scale),
        grid_spec=grid_spec,
        out_shape=jax.ShapeDtypeStruct((nq * tq, n_heads * LANES), BF16),
        compiler_params=_params(("arbitrary", "arbitrary")),
        name=name,
    )(shift_is_safe.astype(jnp.int32).reshape(1), qt, k, vl, vc, *extra)


def _merge_kernel(x_ref, mod_ref, gain_ref, wg_ref, oa_ref, of_ref, or_ref, gb_ref, oc_ref, od_ref,
                  wa_ref, wb_ref, wc_ref, wd_ref, wo_ref, out_ref):
    x = x_ref[...]
    mod = mod_ref[0]
    hb = _prenorm(x, gain_ref[...], mod, 0).astype(BF16)
    ob = of_ref[...] + or_ref[...]
    g = gb_ref[...]
    parts = [_rms(ob[:, h * LANES:(h + 1) * LANES], float(LANES)) for h in range(N_HEADS_RET)]
    bb = (jnp.concatenate(parts, axis=-1) * _silu(g)).astype(BF16)
    branches = ((oa_ref[...], wa_ref), (bb, wb_ref), (oc_ref[...], wc_ref), (od_ref[...], wd_ref))
    acc = None
    for n, (b, w_ref) in enumerate(branches):
        gate = _sigmoid(_dot(hb, wg_ref[:, n * D_MODEL:(n + 1) * D_MODEL]))
        term = gate * _dot(b, w_ref[...])
        acc = term if acc is None else acc + term
    y = _dot(acc.astype(BF16), wo_ref[...])
    out_ref[...] = x + mod[2:3, :] * y


def _merge_call(xall, modsel, gain, wl, oa, of, orv, gb, oc, od, *, n_rows, n_lat):
    nt = n_rows // ROW_TILE
    row = lambda w: pl.BlockSpec((ROW_TILE, w), lambda i: (i, 0))
    return pl.pallas_call(
        _merge_kernel,
        grid=(nt,),
        in_specs=[row(D_MODEL),
                  _mod_spec(n_lat // ROW_TILE),
                  _const_spec((1, D_MODEL)),
                  _const_spec((D_MODEL, 4 * D_MODEL)),
                  row(1024), row(512), row(512), row(512), row(1024), row(512),
                  _const_spec((1024, D_MODEL)), _const_spec((512, D_MODEL)),
                  _const_spec((1024, D_MODEL)), _const_spec((512, D_MODEL)),
                  _const_spec((D_MODEL, D_MODEL))],
        out_specs=row(D_MODEL),
        out_shape=jax.ShapeDtypeStruct((n_rows, D_MODEL), F32),
        compiler_params=_params(("arbitrary",)),
        name="merge",
    )(xall, modsel, gain, wl["w_gate"], oa, of, orv, gb, oc, od,
      wl["wb_a"], wl["wb_b"], wl["wb_c"], wl["wb_d"], wl["w_out"])


def _ffn_kernel(x_ref, mod_ref, gain_ref, wi_ref, wo_ref, out_ref):
    x = x_ref[...]
    mod = mod_ref[0]
    hb = _prenorm(x, gain_ref[...], mod, 3).astype(BF16)
    gate = _dot(hb, wi_ref[:, :FFN_HIDDEN])
    up = _dot(hb, wi_ref[:, FFN_HIDDEN:])
    y = _dot((_silu(gate) * up).astype(BF16), wo_ref[...])
    out_ref[...] = x + mod[5:6, :] * y


def _ffn_call(x1, modsel, gain, wl, n_lat):
    r = x1.shape[0]
    nt = r // ROW_TILE
    row = pl.BlockSpec((ROW_TILE, D_MODEL), lambda i: (i, 0))
    return pl.pallas_call(
        _ffn_kernel,
        grid=(nt,),
        in_specs=[row, _mod_spec(n_lat // ROW_TILE),
                  _const_spec((1, D_MODEL)),
                  _const_spec((D_MODEL, 2 * FFN_HIDDEN)),
                  _const_spec((FFN_HIDDEN, D_MODEL))],
        out_specs=row,
        out_shape=jax.ShapeDtypeStruct((r, D_MODEL), F32),
        compiler_params=_params(("arbitrary",)),
        name="ffn",
    )(x1, modsel, gain, wl["w_ffn_in"], wl["w_ffn_out"])


def _pad_rows_per_head(w, n_heads, width):
    w = w.reshape(n_heads, width, w.shape[-1])
    w = jnp.pad(w, ((0, 0), (0, LANES - width), (0, 0)))
    return w.reshape(n_heads * LANES, -1)


def _pad_cols_per_head(w, n_heads, width):
    w = w.reshape(w.shape[0], n_heads, width)
    w = jnp.pad(w, ((0, 0), (0, 0), (0, LANES - width)))
    return w.reshape(w.shape[0], n_heads * LANES)


def _pad_vec(v):
    return jnp.pad(v, (0, LANES - v.shape[0]))


def _layer_weights(l, w_in, gqa_qk_gain, mla_cq_gain, mla_ckv_gain, mla_w_uq, mla_w_ukv,
                   mla_qk_gain, diff_qk_gain, w_branch, w_out, w_ffn_in, w_ffn_out):
    wi = w_in[l]
    w_main = jnp.concatenate([wi[:, :2976], jnp.zeros((D_MODEL, 96), F32), wi[:, 2976:4512]], axis=1)
    ukv = mla_w_ukv[l].reshape(256, 8, 128)
    head_gain = jnp.stack([
        _pad_vec(gqa_qk_gain[l, 0]), _pad_vec(gqa_qk_gain[l, 1]),
        _pad_vec(mla_qk_gain[l, 0]), _pad_vec(mla_qk_gain[l, 1]),
        _pad_vec(diff_qk_gain[l, 0]), _pad_vec(diff_qk_gain[l, 1]),
        jnp.zeros((LANES,), F32), jnp.zeros((LANES,), F32)])
    wb = w_branch[l]
    return dict(
        w_main=w_main.astype(BF16),
        w_gate=wi[:, 4512:].astype(BF16),
        wuq=_pad_cols_per_head(mla_w_uq[l], 8, 96).astype(BF16),
        wuk=_pad_cols_per_head(ukv[:, :, :64].reshape(256, 512), 8, 64).astype(BF16),
        wuv=_pad_cols_per_head(ukv[:, :, 64:].reshape(256, 512), 8, 64).astype(BF16),
        head_gain=head_gain,
        cq_gain=mla_cq_gain[l][None], ckv_gain=mla_ckv_gain[l][None],
        wb_a=_pad_rows_per_head(wb[0], 8, 64).astype(BF16),
        wb_b=wb[1].astype(BF16),
        wb_c=_pad_rows_per_head(wb[2], 8, 64).astype(BF16),
        wb_d=wb[3].astype(BF16),
        w_out=w_out[l].astype(BF16),
        w_ffn_in=w_ffn_in[l].astype(BF16),
        w_ffn_out=w_ffn_out[l].astype(BF16),
    )


def _score_bounds(gqa_gain, mla_gain, diff_gain):
    amax = lambda g: jnp.max(jnp.abs(g.astype(F32)))
    margin = 1.01
    b_gqa = 64 * amax(gqa_gain[0]) * amax(gqa_gain[1]) * 64 ** -0.5
    b_diff = 64 * amax(diff_gain[0]) * amax(diff_gain[1]) * 64 ** -0.5
    norm = lambda g: jnp.sqrt(64 * amax(g[:64]) ** 2 + 32 * amax(g[64:]) ** 2)
    b_mla = norm(mla_gain[0]) * norm(mla_gain[1]) * 96 ** -0.5
    return jnp.stack([b_gqa, b_mla, b_diff]) * (LOG2E * margin)


def _rope_tables(seq, n_rest):
    n_rows = seq // GRID_W
    row = jnp.repeat(jnp.arange(n_rows, dtype=F32), GRID_W)
    col = jnp.tile(jnp.arange(GRID_W, dtype=F32), n_rows)

    def pairs(dim):
        quarter = dim // 4
        freqs = ROPE_THETA ** (-jnp.arange(quarter, dtype=F32) / quarter)
        ang = jnp.concatenate([row[:, None] * freqs, col[:, None] * freqs], axis=-1)
        cos = jnp.repeat(jnp.cos(ang), 2, axis=-1)
        sin = jnp.repeat(jnp.sin(ang), 2, axis=-1) * jnp.tile(jnp.array([-1.0, 1.0], F32), dim // 2)
        return cos, sin

    c64, s64 = pairs(64)
    c32, s32 = pairs(32)
    one = lambda w: jnp.ones((seq, w), F32)
    zero = lambda w: jnp.zeros((seq, w), F32)
    t64c = jnp.concatenate([c64, c64], axis=-1)
    t64s = jnp.concatenate([s64, s64], axis=-1)
    tmc = jnp.concatenate([one(64), c32, one(32)], axis=-1)
    tms = jnp.concatenate([zero(64), s32, zero(32)], axis=-1)
    rest_c = jnp.ones((n_rest, LANES), F32)
    rest_s = jnp.zeros((n_rest, LANES), F32)
    cat = lambda t, rest: jnp.concatenate([t, rest], axis=0)
    return cat(t64c, rest_c), cat(t64s, rest_s), cat(tmc, rest_c), cat(tms, rest_s)


def kernel(x, c, ctx, c_ctx, w_mod, b_mod, norm_gain, w_in, gqa_qk_gain, ret_decay, mla_cq_gain,
           mla_ckv_gain, mla_w_uq, mla_w_ukv, mla_qk_gain, diff_qk_gain, diff_lambda, diff_subln_gain,
           w_branch, w_out, w_ffn_in, w_ffn_out):
    depth = w_mod.shape[0]
    batch, seq, _ = x.shape
    assert batch == 1 and ctx.shape[1] == CTX and seq % (STAGES_PER_TRIP * KV_CHUNK) == 0
    n_all = seq + CTX + PAD

    tabs = _rope_tables(seq, CTX + PAD)
    cc = jnp.concatenate([c_ctx[None], c, jnp.zeros((6, D_MODEL), F32)], axis=0)
    mods = _mod_call(cc, w_mod, b_mod)
    xall = jnp.concatenate([x[0], ctx[0], jnp.zeros((PAD, D_MODEL), F32)], axis=0)

    for l in range(depth):
        last = l == depth - 1
        lam_init = 0.8 - 0.6 * math.exp(-0.3 * l)
        wl = _layer_weights(l, w_in, gqa_qk_gain, mla_cq_gain, mla_ckv_gain, mla_w_uq, mla_w_ukv,
                            mla_qk_gain, diff_qk_gain, w_branch, w_out, w_ffn_in, w_ffn_out)
        modsel = jnp.pad(mods[l, :2].reshape(2, 6, D_MODEL), ((0, 0), (0, 2), (0, 0)))
        g_attn = norm_gain[l, 0][None]
        g_ffn = norm_gain[l, 1][None]

        bounds = _score_bounds(gqa_qk_gain[l], mla_qk_gain[l], diff_qk_gain[l])
        safe = bounds <= SHIFT_MAX
        shift_tab = jnp.pad(jnp.broadcast_to(-jnp.minimum(bounds, SHIFT_MAX)[:, None], (3, LANES)),
                            ((0, 5), (0, 0)))
        p = _prep_call(xall, modsel, g_attn, wl, shift_tab, tabs, seq)
        dec = jnp.pad(ret_decay[l].astype(F32), ((0, 6), (0, LANES - N_HEADS_RET)))
        of, orv = _ret_call(p["qb"], p["kb"], p["vb"], dec, seq)
        oa = _attn_call(p["qa"], p["ka"], p["va"], safe[0], n_heads=8, n_kv=2, nm=1, dv=64, n_lat=seq,
                        tq=ATT_TQ, name="attn_gqa")
        oc = _attn_call(p["qc"], p["kc"], p["vc"], safe[1], n_heads=8, n_kv=8, nm=1, dv=64, n_lat=seq,
                        tq=ATT_TQ, name="attn_mla")
        lam_tab = jnp.pad(diff_lambda[l].astype(F32), ((0, 4), (0, LANES - 64)))
        od = _attn_call(p["qd"], p["kd"], p["vd"], safe[2], n_heads=4, n_kv=4, nm=2, dv=128, n_lat=seq,
                        tq=ATT_TQ, name="attn_diff", extra=(lam_tab, diff_subln_gain[l][None]),
                        diff_scale=1.0 - lam_init)
        n_rows = seq if last else n_all
        x1 = _merge_call(xall, modsel, g_attn, wl, oa, of, orv, p["gb"], oc, od, n_rows=n_rows, n_lat=seq)
        xall = _ffn_call(x1, modsel, g_ffn, wl, seq)

    return xall[None]
```

```python
import functools
import math

import jax
import jax.numpy as jnp
from jax import lax
from jax.experimental import pallas as pl
from jax.experimental.pallas import tpu as pltpu

F32 = jnp.float32
BF16 = jnp.bfloat16

D_MODEL = 1024
GRID_W = 64
ROPE_THETA = 10000.0
EPS = 1e-6
RET_CHUNK = 128
N_HEADS_RET = 4
FFN_HIDDEN = 2816
LANES = 128
BF16_ROWS = 16
ROW_TILE = 256
CTX = 256
PAD = 0
ATT_TQ = 2048
Q_BLOCK = 256
KV_CHUNK = 512
STAGES_PER_TRIP = 2
FIXED_SHIFT_STAGES_PER_TRIP = 4
LOG2E = 1.4426950408889634
SHIFT_MAX = 40.0
VMEM_LIMIT = 56 * 1024 * 1024

SEG = dict(a_q=(0, 512), a_k=(512, 640), a_v=(640, 768),
           b_q=(768, 1024), b_k=(1024, 1280), b_v=(1280, 1792), b_g=(1792, 2304),
           c_q=(2304, 2688), c_kv=(2688, 2944), c_kr=(2944, 3072),
           d_q=(3072, 3584), d_k=(3584, 4096), d_v=(4096, 4608))
MAIN_WIDTH = 4608


def _params(sem):
    return pltpu.CompilerParams(dimension_semantics=sem, vmem_limit_bytes=VMEM_LIMIT)


def _const_spec(shape):
    n = len(shape)
    return pl.BlockSpec(shape, lambda *_: (0,) * n, pipeline_mode=pl.Buffered(1))


def _mod_spec(n_lat_tiles):
    return pl.BlockSpec((1, 8, D_MODEL), lambda i: (jnp.where(i < n_lat_tiles, 1, 0), 0, 0))


def _sigmoid(x):
    return 1.0 / (1.0 + jnp.exp(-x))


def _silu(x):
    return x * _sigmoid(x)


def _rms(x, n):
    ss = jnp.sum(x * x, axis=-1, keepdims=True) * (1.0 / n)
    return x * lax.rsqrt(ss + EPS)


def _lane(shape):
    return lax.broadcasted_iota(jnp.int32, shape, 1)


def _rope(y, cos, sin_signed):
    even = (_lane(y.shape) & 1) == 0
    partner = jnp.where(even, pltpu.roll(y, LANES - 1, 1), pltpu.roll(y, 1, 1))
    return y * cos + partner * sin_signed


def _head_slabs(p):
    lo = _lane((p.shape[0], LANES)) < 64
    out = []
    for j in range(p.shape[1] // LANES):
        src = p[:, j * LANES:(j + 1) * LANES]
        out.append(jnp.where(lo, src, 0.0))
        out.append(jnp.where(lo, pltpu.roll(src, 64, 1), 0.0))
    return out


def _dot(a, b):
    return jnp.dot(a, b, preferred_element_type=F32)


def _mod_kernel(c_ref, w_ref, b_ref, o_ref):
    s = _silu(c_ref[...]).astype(BF16)
    o_ref[0] = _dot(s, w_ref[0].astype(BF16)) + b_ref[0]


def _mod_call(cc, w_mod, b_mod):
    depth = w_mod.shape[0]
    nt = 6
    return pl.pallas_call(
        _mod_kernel,
        grid=(depth, nt),
        in_specs=[pl.BlockSpec((8, D_MODEL), lambda l, j: (0, 0)),
                  pl.BlockSpec((1, D_MODEL, D_MODEL), lambda l, j: (l, 0, j)),
                  pl.BlockSpec((1, 1, D_MODEL), lambda l, j: (l, 0, j))],
        out_specs=pl.BlockSpec((1, 8, D_MODEL), lambda l, j: (l, 0, j)),
        out_shape=jax.ShapeDtypeStruct((depth, 8, 6 * D_MODEL), F32),
        compiler_params=_params(("arbitrary", "arbitrary")),
        name="mod",
    )(cc, w_mod, b_mod.reshape(depth, 1, 6 * D_MODEL))


def _prenorm(x, gain, mod, shift_row):
    h = _rms(x, D_MODEL) * gain
    return h * (1.0 + mod[shift_row + 1:shift_row + 2, :]) + mod[shift_row:shift_row + 1, :]


def _prep_kernel(x_ref, mod_ref, gain_ref, w_ref, wuq_ref, wuk_ref, wuv_ref,
                 hg_ref, shift_ref, cqg_ref, ckvg_ref, c64_ref, s64_ref, cm_ref, sm_ref,
                 qa_ref, ka_ref, va_ref, qb_ref, kb_ref, vb_ref, gb_ref,
                 qc_ref, kc_ref, vc_ref, qd_ref, kd_ref, vd_ref):
    hb = _prenorm(x_ref[...], gain_ref[...], mod_ref[0], 0).astype(BF16)
    t = hb.shape[0]
    lane = _lane((t, LANES))
    c64, s64 = c64_ref[...], s64_ref[...]
    cm, sm = cm_ref[...], sm_ref[...]

    def proj(name):
        a, b = SEG[name]
        return _dot(hb, w_ref[:, a:b])

    def put(ref, j, val):
        ref[:, j * LANES:(j + 1) * LANES] = val.astype(ref.dtype)

    def rms64(slab):
        ss = jnp.sum(slab * slab, axis=-1, keepdims=True) * (1.0 / 64.0)
        return slab * lax.rsqrt(ss + EPS)

    def nr64(slab, gain_row, scale):
        y = rms64(slab) * hg_ref[gain_row:gain_row + 1, :]
        y = _rope(y, c64, s64)
        return y * scale if scale != 1.0 else y

    ones_at_64 = lane == 64
    ones_at_96 = lane == 96

    for j, slab in enumerate(_head_slabs(proj("a_q"))):
        put(qa_ref, j, jnp.where(ones_at_64, shift_ref[0:1, :], nr64(slab, 0, 64 ** -0.5 * LOG2E)))
    for j, slab in enumerate(_head_slabs(proj("a_k"))):
        put(ka_ref, j, jnp.where(ones_at_64, 1.0, nr64(slab, 1, 1.0)))
    for j, slab in enumerate(_head_slabs(proj("a_v"))):
        put(va_ref, j, jnp.where(ones_at_64, 1.0, slab))

    q = proj("b_q")
    k = proj("b_k") * (64 ** -0.5)
    for j in range(2):
        put(qb_ref, j, _rope(q[:, j * LANES:(j + 1) * LANES], c64, s64))
        put(kb_ref, j, _rope(k[:, j * LANES:(j + 1) * LANES], c64, s64))
    vb_ref[...] = proj("b_v").astype(vb_ref.dtype)
    gb_ref[...] = proj("b_g")

    nope = lane < 64
    ropel = (lane >= 64) & (lane < 96)
    cq = (_rms(proj("c_q"), 384.0) * cqg_ref[...]).astype(BF16)
    qm = _dot(cq, wuq_ref[...])
    gq = hg_ref[2:3, :]
    for h in range(8):
        slab = qm[:, h * LANES:(h + 1) * LANES]
        sq = slab * slab
        ssn = jnp.sum(jnp.where(nope, sq, 0.0), axis=-1, keepdims=True) * (1.0 / 64.0)
        ssr = jnp.sum(jnp.where(ropel, sq, 0.0), axis=-1, keepdims=True) * (1.0 / 32.0)
        inv = jnp.where(nope, lax.rsqrt(ssn + EPS), lax.rsqrt(ssr + EPS))
        y = _rope(slab * inv * gq, cm, sm) * (96 ** -0.5 * LOG2E)
        put(qc_ref, h, jnp.where(ones_at_96, shift_ref[1:2, :], y))
    ckv = (_rms(proj("c_kv"), 256.0) * ckvg_ref[...]).astype(BF16)
    kn = _dot(ckv, wuk_ref[...])
    vm = _dot(ckv, wuv_ref[...])
    gk = hg_ref[3:4, :]
    kr = pltpu.roll(proj("c_kr"), 64, 1)
    ssk = jnp.sum(kr * kr, axis=-1, keepdims=True) * (1.0 / 32.0)
    kr = _rope(kr * lax.rsqrt(ssk + EPS) * gk, cm, sm)
    kr = jnp.where(ropel, kr, 0.0)
    for h in range(8):
        slab = kn[:, h * LANES:(h + 1) * LANES]
        put(kc_ref, h, jnp.where(ones_at_96, 1.0, jnp.where(nope, rms64(slab) * gk, kr)))
        put(vc_ref, h, jnp.where(ones_at_64, 1.0, vm[:, h * LANES:(h + 1) * LANES]))

    for j, slab in enumerate(_head_slabs(proj("d_q"))):
        put(qd_ref, j, jnp.where(ones_at_64, shift_ref[2:3, :], nr64(slab, 4, 64 ** -0.5 * LOG2E)))
    for j, slab in enumerate(_head_slabs(proj("d_k"))):
        put(kd_ref, j, jnp.where(ones_at_64, 1.0, nr64(slab, 5, 1.0)))
    v = proj("d_v")
    ones_at_0 = jnp.where(lane == 0, 1.0, 0.0)
    for h in range(4):
        put(vd_ref, 2 * h, v[:, h * LANES:(h + 1) * LANES])
        put(vd_ref, 2 * h + 1, ones_at_0)


def _prep_call(xall, modsel, gain, wl, shift_tab, tabs, n_lat):
    r = xall.shape[0]
    nt = r // ROW_TILE
    row = lambda w: pl.BlockSpec((ROW_TILE, w), lambda i: (i, 0))
    out_w = dict(qa=1024, ka=256, va=256, qb=256, kb=256, vb=512, gb=512,
                 qc=1024, kc=1024, vc=1024, qd=1024, kd=1024, vd=1024)
    out_shape = [jax.ShapeDtypeStruct((r, w), F32 if n == "gb" else BF16) for n, w in out_w.items()]
    outs = pl.pallas_call(
        _prep_kernel,
        grid=(nt,),
        in_specs=[row(D_MODEL),
                  _mod_spec(n_lat // ROW_TILE),
                  _const_spec((1, D_MODEL)),
                  _const_spec((D_MODEL, MAIN_WIDTH)),
                  _const_spec((384, 1024)), _const_spec((256, 1024)), _const_spec((256, 1024)),
                  _const_spec((8, LANES)), _const_spec((8, LANES)), _const_spec((1, 384)), _const_spec((1, 256)),
                  row(LANES), row(LANES), row(LANES), row(LANES)],
        out_specs=[row(w) for w in out_w.values()],
        out_shape=out_shape,
        compiler_params=_params(("arbitrary",)),
        name="prep",
    )(xall, modsel, gain, wl["w_main"], wl["wuq"], wl["wuk"], wl["wuv"],
      wl["head_gain"], shift_tab, wl["cq_gain"], wl["ckv_gain"], *tabs)
    return dict(zip(out_w.keys(), outs))


N_PAD_CHUNKS = PAD // RET_CHUNK


def _ret_kernel(dec_ref, qf_ref, kf_ref, vf_ref, qr_ref, kr_ref, vr_ref,
                of_ref, or_ref, state_ref, intra_ref, qdec_ref, kdec_ref, cdec_ref):
    c = RET_CHUNK
    t = pl.program_id(0)

    @pl.when(t == 0)
    def _tables():
        x = dec_ref[...]
        lg = jnp.minimum(x, 0.0) - jnp.log(1.0 + jnp.exp(-jnp.abs(x)))
        i_idx = lax.broadcasted_iota(jnp.int32, (c, c), 0).astype(F32)
        j_idx = lax.broadcasted_iota(jnp.int32, (c, c), 1).astype(F32)
        for d in range(2):
            for h in range(N_HEADS_RET):
                g = lg[d:d + 1, h:h + 1]
                if d == 0:
                    rel, qe, ke = i_idx - j_idx, i_idx + 1.0, c - 1.0 - i_idx
                else:
                    rel, qe, ke = j_idx - i_idx, c - i_idx, i_idx
                intra_ref[d, h] = jnp.where(rel >= 0, jnp.exp(g * jnp.maximum(rel, 0.0)), 0.0)
                qdec_ref[d, h] = jnp.exp(g * qe)
                kdec_ref[d, h] = jnp.exp(g * ke[:, :64])
                cdec_ref[d, h] = jnp.exp(jnp.broadcast_to(g, (8, LANES)) * float(c))

    @pl.when((t == 0) | (t == N_PAD_CHUNKS))
    def _clear():
        state_ref[...] = jnp.zeros_like(state_ref)

    for d, (q_ref, k_ref, v_ref, o_ref) in enumerate(
            ((qf_ref, kf_ref, vf_ref, of_ref), (qr_ref, kr_ref, vr_ref, or_ref))):
        for h in range(N_HEADS_RET):
            qh = q_ref[:, h * 64:(h + 1) * 64]
            kh = k_ref[:, h * 64:(h + 1) * 64]
            vh = v_ref[:, h * LANES:(h + 1) * LANES]
            sc = lax.dot_general(qh, kh, (((1,), (1,)), ((), ())), preferred_element_type=F32)
            sc = sc * intra_ref[d, h]
            st = state_ref[d, h]
            o = _dot(sc.astype(BF16), vh) + _dot(qh, st.astype(BF16)) * qdec_ref[d, h]
            o_ref[:, h * LANES:(h + 1) * LANES] = o
            kd = (kh.astype(F32) * kdec_ref[d, h]).astype(BF16)
            upd = lax.dot_general(kd, vh, (((0,), (0,)), ((), ())), preferred_element_type=F32)
            state_ref[d, h] = st * cdec_ref[d, h][0:1, :] + upd


def _ret_call(qb, kb, vb, dec, n_lat):
    r = qb.shape[0]
    nc = r // RET_CHUNK
    nl = n_lat // RET_CHUNK
    nctx = CTX // RET_CHUNK
    npad = N_PAD_CHUNKS
    assert nc == nl + nctx + npad
    fwd = lambda t: (jnp.where(t < npad, nl + nctx + t,
                               jnp.where(t < npad + nctx, nl + t - npad, t - npad - nctx)), 0)
    bwd = lambda t: (jnp.where(t < npad, nl + nctx + t, nc - 1 - t), 0)
    blk = lambda w, m: pl.BlockSpec((RET_CHUNK, w), m)
    return pl.pallas_call(
        _ret_kernel,
        grid=(nc,),
        in_specs=[_const_spec((8, LANES)),
                  blk(256, fwd), blk(256, fwd), blk(512, fwd),
                  blk(256, bwd), blk(256, bwd), blk(512, bwd)],
        out_specs=[blk(512, fwd), blk(512, bwd)],
        out_shape=[jax.ShapeDtypeStruct((r, 512), F32)] * 2,
        scratch_shapes=[pltpu.VMEM((2, N_HEADS_RET, 64, LANES), F32),
                        pltpu.VMEM((2, N_HEADS_RET, RET_CHUNK, RET_CHUNK), F32),
                        pltpu.VMEM((2, N_HEADS_RET, RET_CHUNK, RET_CHUNK), F32),
                        pltpu.VMEM((2, N_HEADS_RET, RET_CHUNK, 64), F32),
                        pltpu.VMEM((2, N_HEADS_RET, 8, LANES), F32)],
        compiler_params=_params(("arbitrary",)),
        name="retention",
    )(dec, qb, kb, vb, qb, kb, vb)


def _attn_kernel(*refs, nm, dv, n_lat, diff_scale):
    if diff_scale is None:
        qt_ref, k_ref, vl_ref, vc_ref, o_ref, m_ref, acc_ref, s_ref, p_ref, mx_ref, alpha_ref = refs
    else:
        qt_ref, k_ref, vl_ref, vc_ref, lam_ref, sg_ref, o_ref, m_ref, acc_ref, s_ref, p_ref, mx_ref, alpha_ref = refs
    i = pl.program_id(1)
    n_chunks = n_lat // KV_CHUNK
    tq = qt_ref.shape[1]
    maps = range(nm)

    def scores(k, a):
        return _dot(k[:, a * LANES:(a + 1) * LANES], qt_ref[a * LANES:(a + 1) * LANES, :])

    kc = k_ref[n_lat:n_lat + CTX, :]
    for a in maps:
        s = scores(kc, a)
        m = jnp.max(s, axis=0, keepdims=True)
        m_ref[a] = m
        acc_ref[a] = _dot(vc_ref[0], jnp.exp2(s - m).astype(BF16))

    def qk(j, slot):
        k = k_ref[pl.ds(pl.multiple_of(j * KV_CHUNK, KV_CHUNK), KV_CHUNK), :]
        for a in maps:
            s = scores(k, a)
            s_ref[a, slot] = s
            mx_ref[a, slot] = jnp.max(s, axis=0, keepdims=True)

    @pl.when(i < n_lat // tq)
    def _latent_keys():
        qk(0, 0)
        p_ref[:, 1] = jnp.zeros((nm,) + p_ref.shape[2:], BF16)
        alpha_ref[...] = jnp.ones_like(alpha_ref)

        def stage(j, slot, prefetch=True, softmax=True):
            other = 1 - slot
            if isinstance(j, int):
                j_prev, k_start = max(j - 1, 0), (j + 1) * KV_CHUNK
            else:
                j_prev, k_start = jnp.maximum(j - 1, 0), pl.multiple_of((j + 1) * KV_CHUNK, KV_CHUNK)
            k_rows = pl.ds(k_start, KV_CHUNK)
            for c in range(tq // Q_BLOCK):
                cols = slice(c * Q_BLOCK, (c + 1) * Q_BLOCK)
                for a in maps:
                    heads = slice(a * LANES, (a + 1) * LANES)
                    if softmax:
                        m_old = m_ref[a, :, cols]
                        m_new = jnp.maximum(m_old, mx_ref[a, slot, :, cols])
                        p_ref[a, slot, :, cols] = jnp.exp2(s_ref[a, slot, :, cols] - m_new).astype(BF16)
                        m_ref[a, :, cols] = m_new
                    pv_c = _dot(vl_ref[0, j_prev], p_ref[a, other, :, cols])
                    if prefetch:
                        s = _dot(k_ref[k_rows, heads], qt_ref[heads, cols])
                        s_ref[a, other, :, cols] = s
                        mx_ref[a, other, :, cols] = jnp.max(s, axis=0, keepdims=True)
                    acc_ref[a, :, cols] = acc_ref[a, :, cols] * alpha_ref[a, :, cols] + pv_c
                    if softmax:
                        alpha_ref[a, :, cols] = jnp.exp2(m_old - m_new)

        def trip(t, carry):
            for u in range(STAGES_PER_TRIP):
                stage(STAGES_PER_TRIP * t + u, u % 2)
            return carry

        n_trips = n_chunks // STAGES_PER_TRIP - 1
        lax.fori_loop(0, n_trips, trip, 0)
        for j in range(n_trips * STAGES_PER_TRIP, n_chunks):
            stage(j, j % 2, prefetch=j + 1 < n_chunks)
        stage(n_chunks, n_chunks % 2, prefetch=False, softmax=False)

    _attn_finish(acc_ref, o_ref, None if diff_scale is None else (lam_ref, sg_ref), dv, diff_scale)


def _attn_finish(acc_ref, o_ref, diff_refs, dv, diff_scale):
    def normalised(a):
        acc = acc_ref[a]
        return acc[:dv] * (1.0 / acc[dv:dv + 1])

    if diff_scale is None:
        o = normalised(0)
        o = jnp.concatenate([o, jnp.zeros((LANES - dv, o.shape[1]), F32)], axis=0)
        o_ref[...] = o.T.astype(o_ref.dtype)
    else:
        lam_ref, sg_ref = diff_refs
        lv = lam_ref[...]
        lam_init = 1.0 - diff_scale
        lam = (jnp.exp(jnp.sum(lv[0:1] * lv[1:2], axis=-1, keepdims=True))
               - jnp.exp(jnp.sum(lv[2:3] * lv[3:4], axis=-1, keepdims=True)) + lam_init)
        o = (normalised(0) - lam * normalised(1)).T
        o_ref[...] = (_rms(o, float(dv)) * sg_ref[...] * diff_scale).astype(o_ref.dtype)


def _attn_fixed_shift_kernel(*refs, nm, dv, n_lat, diff_scale):
    if diff_scale is None:
        qt_ref, k_ref, vl_ref, vc_ref, o_ref, acc_ref, p_ref, l_ref = refs
        diff_refs = None
    else:
        qt_ref, k_ref, vl_ref, vc_ref, lam_ref, sg_ref, o_ref, acc_ref, p_ref, l_ref = refs
        diff_refs = (lam_ref, sg_ref)
    i = pl.program_id(1)
    n_chunks = n_lat // KV_CHUNK
    tq = qt_ref.shape[1]
    maps = range(nm)
    blocks = [(slice(c * Q_BLOCK, (c + 1) * Q_BLOCK), a, slice(a * LANES, (a + 1) * LANES))
              for c in range(tq // Q_BLOCK) for a in maps]

    def probs(k_rows, heads, cols):
        p = jnp.exp2(_dot(k_ref[k_rows, heads], qt_ref[heads, cols]))
        return p.astype(BF16), jnp.sum(p.reshape(p.shape[0] // 8, 8, p.shape[1]), axis=0)

    def context_keys(cols, a, heads):
        p, l8 = probs(slice(n_lat, n_lat + CTX), heads, cols)
        acc_ref[a, :dv, cols] = _dot(vc_ref[0, :dv, :], p)
        l_ref[a, :, cols] = l8

    def store_row_sums():
        for a in maps:
            acc_ref[a, dv:dv + 1, :] = jnp.sum(l_ref[a], axis=0, keepdims=True)

    @pl.when(i >= n_lat // tq)
    def _context_queries():
        for block in blocks:
            context_keys(*block)
        store_row_sums()

    @pl.when(i < n_lat // tq)
    def _latent_queries():
        for cols, a, heads in blocks:
            context_keys(cols, a, heads)
            p_ref[a, 0, :, cols], l8 = probs(slice(0, KV_CHUNK), heads, cols)
            l_ref[a, :, cols] = l_ref[a, :, cols] + l8

        def stage(j, slot, prefetch=True):
            if isinstance(j, int):
                k_start = (j + 1) * KV_CHUNK
            else:
                k_start = pl.multiple_of((j + 1) * KV_CHUNK, KV_CHUNK)
            for cols, a, heads in blocks:
                pv = _dot(vl_ref[0, j, :dv, :], p_ref[a, slot, :, cols])
                if prefetch:
                    p_ref[a, 1 - slot, :, cols], l8 = probs(pl.ds(k_start, KV_CHUNK), heads, cols)
                    l_ref[a, :, cols] = l_ref[a, :, cols] + l8
                acc_ref[a, :dv, cols] = acc_ref[a, :dv, cols] + pv

        def trip(t, carry):
            for u in range(FIXED_SHIFT_STAGES_PER_TRIP):
                stage(FIXED_SHIFT_STAGES_PER_TRIP * t + u, u % 2)
            return carry

        n_trips = n_chunks // FIXED_SHIFT_STAGES_PER_TRIP - 1
        lax.fori_loop(0, n_trips, trip, 0)
        for j in range(n_trips * FIXED_SHIFT_STAGES_PER_TRIP, n_chunks):
            stage(j, j % 2, prefetch=j + 1 < n_chunks)
        store_row_sums()

    _attn_finish(acc_ref, o_ref, diff_refs, dv, diff_scale)


def _attn_dispatch_kernel(flag_ref, *refs, nm, dv, n_lat, diff_scale):
    n_in = 4 if diff_scale is None else 6
    ins, o_ref = refs[:n_in], refs[n_in]
    m_ref, acc_ref, s_ref, p_ref, mx_ref, alpha_ref, l_ref = refs[n_in + 1:]
    static = dict(nm=nm, dv=dv, n_lat=n_lat, diff_scale=diff_scale)

    @pl.when(flag_ref[0] != 0)
    def _fixed_shift():
        _attn_fixed_shift_kernel(*ins, o_ref, acc_ref, p_ref, l_ref, **static)

    @pl.when(flag_ref[0] == 0)
    def _running_max():
        _attn_kernel(*ins, o_ref, m_ref, acc_ref, s_ref, p_ref, mx_ref, alpha_ref, **static)


def _attn_call(q, k, v, shift_is_safe, *, n_heads, n_kv, nm, dv, n_lat, tq, name, extra=(), diff_scale=None):
    r = q.shape[0]
    assert n_lat % tq == 0 and r - n_lat == CTX <= tq
    nq = n_lat // tq + 1
    group = n_heads // n_kv
    dvp = -(-(dv + 1) // BF16_ROWS) * BF16_ROWS
    slab = v.shape[1] // n_kv
    n_chunks = n_lat // KV_CHUNK
    qt = jnp.pad(q.T, ((0, 0), (0, nq * tq - r)))
    v3 = v.reshape(r, n_kv, slab)[:, :, :dvp]
    vl = v3[:n_lat].reshape(n_chunks, KV_CHUNK, n_kv, dvp).transpose(2, 0, 3, 1)
    vc = v3[n_lat:].transpose(1, 2, 0)
    extra_specs = [pl.BlockSpec(e.shape, lambda h, i, f: (0, 0)) for e in extra]
    once = pl.Buffered(1)
    grid_spec = pltpu.PrefetchScalarGridSpec(
        num_scalar_prefetch=1,
        grid=(n_heads, nq),
        in_specs=[pl.BlockSpec((nm * LANES, tq), lambda h, i, f: (h, i)),
                  pl.BlockSpec((r, nm * LANES), lambda h, i, f: (0, h // group), pipeline_mode=once),
                  pl.BlockSpec((1, n_chunks, dvp, KV_CHUNK), lambda h, i, f: (h // group, 0, 0, 0),
                               pipeline_mode=once),
                  pl.BlockSpec((1, dvp, CTX), lambda h, i, f: (h // group, 0, 0))] + extra_specs,
        out_specs=pl.BlockSpec((tq, LANES), lambda h, i, f: (i, h)),
        scratch_shapes=[pltpu.VMEM((nm, 1, tq), F32),
                        pltpu.VMEM((nm, dvp, tq), F32),
                        pltpu.VMEM((nm, 2, KV_CHUNK, tq), F32),
                        pltpu.VMEM((nm, 2, KV_CHUNK, tq), BF16),
                        pltpu.VMEM((nm, 2, 1, tq), F32),
                        pltpu.VMEM((nm, 1, tq), F32),
                        pltpu.VMEM((nm, 8, tq), F32)])
    return pl.pallas_call(
        functools.partial(_attn_dispatch_kernel, nm=nm, dv=dv, n_lat=n_lat, diff_scale=diff_scale),
        grid_spec=grid_spec,
        out_shape=jax.ShapeDtypeStruct((nq * tq, n_heads * LANES), BF16),
        compiler_params=_params(("arbitrary", "arbitrary")),
        name=name,
    )(shift_is_safe.astype(jnp.int32).reshape(1), qt, k, vl, vc, *extra)


def _merge_kernel(x_ref, mod_ref, gain_ref, wg_ref, oa_ref, of_ref, or_ref, gb_ref, oc_ref, od_ref,
                  wa_ref, wb_ref, wc_ref, wd_ref, wo_ref, out_ref):
    x = x_ref[...]
    mod = mod_ref[0]
    hb = _prenorm(x, gain_ref[...], mod, 0).astype(BF16)
    ob = of_ref[...] + or_ref[...]
    g = gb_ref[...]
    parts = [_rms(ob[:, h * LANES:(h + 1) * LANES], float(LANES)) for h in range(N_HEADS_RET)]
    bb = (jnp.concatenate(parts, axis=-1) * _silu(g)).astype(BF16)
    branches = ((oa_ref[...], wa_ref), (bb, wb_ref), (oc_ref[...], wc_ref), (od_ref[...], wd_ref))
    acc = None
    for n, (b, w_ref) in enumerate(branches):
        gate = _sigmoid(_dot(hb, wg_ref[:, n * D_MODEL:(n + 1) * D_MODEL]))
        term = gate * _dot(b, w_ref[...])
        acc = term if acc is None else acc + term
    y = _dot(acc.astype(BF16), wo_ref[...])
    out_ref[...] = x + mod[2:3, :] * y


def _merge_call(xall, modsel, gain, wl, oa, of, orv, gb, oc, od, *, n_rows, n_lat):
    nt = n_rows // ROW_TILE
    row = lambda w: pl.BlockSpec((ROW_TILE, w), lambda i: (i, 0))
    return pl.pallas_call(
        _merge_kernel,
        grid=(nt,),
        in_specs=[row(D_MODEL),
                  _mod_spec(n_lat // ROW_TILE),
                  _const_spec((1, D_MODEL)),
                  _const_spec((D_MODEL, 4 * D_MODEL)),
                  row(1024), row(512), row(512), row(512), row(1024), row(512),
                  _const_spec((1024, D_MODEL)), _const_spec((512, D_MODEL)),
                  _const_spec((1024, D_MODEL)), _const_spec((512, D_MODEL)),
                  _const_spec((D_MODEL, D_MODEL))],
        out_specs=row(D_MODEL),
        out_shape=jax.ShapeDtypeStruct((n_rows, D_MODEL), F32),
        compiler_params=_params(("arbitrary",)),
        name="merge",
    )(xall, modsel, gain, wl["w_gate"], oa, of, orv, gb, oc, od,
      wl["wb_a"], wl["wb_b"], wl["wb_c"], wl["wb_d"], wl["w_out"])


def _ffn_kernel(x_ref, mod_ref, gain_ref, wi_ref, wo_ref, out_ref):
    x = x_ref[...]
    mod = mod_ref[0]
    hb = _prenorm(x, gain_ref[...], mod, 3).astype(BF16)
    gate = _dot(hb, wi_ref[:, :FFN_HIDDEN])
    up = _dot(hb, wi_ref[:, FFN_HIDDEN:])
    y = _dot((_silu(gate) * up).astype(BF16), wo_ref[...])
    out_ref[...] = x + mod[5:6, :] * y


def _ffn_call(x1, modsel, gain, wl, n_lat):
    r = x1.shape[0]
    nt = r // ROW_TILE
    row = pl.BlockSpec((ROW_TILE, D_MODEL), lambda i: (i, 0))
    return pl.pallas_call(
        _ffn_kernel,
        grid=(nt,),
        in_specs=[row, _mod_spec(n_lat // ROW_TILE),
                  _const_spec((1, D_MODEL)),
                  _const_spec((D_MODEL, 2 * FFN_HIDDEN)),
                  _const_spec((FFN_HIDDEN, D_MODEL))],
        out_specs=row,
        out_shape=jax.ShapeDtypeStruct((r, D_MODEL), F32),
        compiler_params=_params(("arbitrary",)),
        name="ffn",
    )(x1, modsel, gain, wl["w_ffn_in"], wl["w_ffn_out"])


def _pad_rows_per_head(w, n_heads, width):
    w = w.reshape(n_heads, width, w.shape[-1])
    w = jnp.pad(w, ((0, 0), (0, LANES - width), (0, 0)))
    return w.reshape(n_heads * LANES, -1)


def _pad_cols_per_head(w, n_heads, width):
    w = w.reshape(w.shape[0], n_heads, width)
    w = jnp.pad(w, ((0, 0), (0, 0), (0, LANES - width)))
    return w.reshape(w.shape[0], n_heads * LANES)


def _pad_vec(v):
    return jnp.pad(v, (0, LANES - v.shape[0]))


def _layer_weights(l, w_in, gqa_qk_gain, mla_cq_gain, mla_ckv_gain, mla_w_uq, mla_w_ukv,
                   mla_qk_gain, diff_qk_gain, w_branch, w_out, w_ffn_in, w_ffn_out):
    wi = w_in[l]
    w_main = jnp.concatenate([wi[:, :2976], jnp.zeros((D_MODEL, 96), F32), wi[:, 2976:4512]], axis=1)
    ukv = mla_w_ukv[l].reshape(256, 8, 128)
    head_gain = jnp.stack([
        _pad_vec(gqa_qk_gain[l, 0]), _pad_vec(gqa_qk_gain[l, 1]),
        _pad_vec(mla_qk_gain[l, 0]), _pad_vec(mla_qk_gain[l, 1]),
        _pad_vec(diff_qk_gain[l, 0]), _pad_vec(diff_qk_gain[l, 1]),
        jnp.zeros((LANES,), F32), jnp.zeros((LANES,), F32)])
    wb = w_branch[l]
    return dict(
        w_main=w_main.astype(BF16),
        w_gate=wi[:, 4512:].astype(BF16),
        wuq=_pad_cols_per_head(mla_w_uq[l], 8, 96).astype(BF16),
        wuk=_pad_cols_per_head(ukv[:, :, :64].reshape(256, 512), 8, 64).astype(BF16),
        wuv=_pad_cols_per_head(ukv[:, :, 64:].reshape(256, 512), 8, 64).astype(BF16),
        head_gain=head_gain,
        cq_gain=mla_cq_gain[l][None], ckv_gain=mla_ckv_gain[l][None],
        wb_a=_pad_rows_per_head(wb[0], 8, 64).astype(BF16),
        wb_b=wb[1].astype(BF16),
        wb_c=_pad_rows_per_head(wb[2], 8, 64).astype(BF16),
        wb_d=wb[3].astype(BF16),
        w_out=w_out[l].astype(BF16),
        w_ffn_in=w_ffn_in[l].astype(BF16),
        w_ffn_out=w_ffn_out[l].astype(BF16),
    )


def _score_bounds(gqa_gain, mla_gain, diff_gain):
    amax = lambda g: jnp.max(jnp.abs(g.astype(F32)))
    margin = 1.01
    b_gqa = 64 * amax(gqa_gain[0]) * amax(gqa_gain[1]) * 64 ** -0.5
    b_diff = 64 * amax(diff_gain[0]) * amax(diff_gain[1]) * 64 ** -0.5
    norm = lambda g: jnp.sqrt(64 * amax(g[:64]) ** 2 + 32 * amax(g[64:]) ** 2)
    b_mla = norm(mla_gain[0]) * norm(mla_gain[1]) * 96 ** -0.5
    return jnp.stack([b_gqa, b_mla, b_diff]) * (LOG2E * margin)


def _rope_tables(seq, n_rest):
    n_rows = seq // GRID_W
    row = jnp.repeat(jnp.arange(n_rows, dtype=F32), GRID_W)
    col = jnp.tile(jnp.arange(GRID_W, dtype=F32), n_rows)

    def pairs(dim):
        quarter = dim // 4
        freqs = ROPE_THETA ** (-jnp.arange(quarter, dtype=F32) / quarter)
        ang = jnp.concatenate([row[:, None] * freqs, col[:, None] * freqs], axis=-1)
        cos = jnp.repeat(jnp.cos(ang), 2, axis=-1)
        sin = jnp.repeat(jnp.sin(ang), 2, axis=-1) * jnp.tile(jnp.array([-1.0, 1.0], F32), dim // 2)
        return cos, sin

    c64, s64 = pairs(64)
    c32, s32 = pairs(32)
    one = lambda w: jnp.ones((seq, w), F32)
    zero = lambda w: jnp.zeros((seq, w), F32)
    t64c = jnp.concatenate([c64, c64], axis=-1)
    t64s = jnp.concatenate([s64, s64], axis=-1)
    tmc = jnp.concatenate([one(64), c32, one(32)], axis=-1)
    tms = jnp.concatenate([zero(64), s32, zero(32)], axis=-1)
    rest_c = jnp.ones((n_rest, LANES), F32)
    rest_s = jnp.zeros((n_rest, LANES), F32)
    cat = lambda t, rest: jnp.concatenate([t, rest], axis=0)
    return cat(t64c, rest_c), cat(t64s, rest_s), cat(tmc, rest_c), cat(tms, rest_s)


def kernel(x, c, ctx, c_ctx, w_mod, b_mod, norm_gain, w_in, gqa_qk_gain, ret_decay, mla_cq_gain,
           mla_ckv_gain, mla_w_uq, mla_w_ukv, mla_qk_gain, diff_qk_gain, diff_lambda, diff_subln_gain,
           w_branch, w_out, w_ffn_in, w_ffn_out):
    depth = w_mod.shape[0]
    batch, seq, _ = x.shape
    assert batch == 1 and ctx.shape[1] == CTX and seq % (STAGES_PER_TRIP * KV_CHUNK) == 0
    n_all = seq + CTX + PAD

    tabs = _rope_tables(seq, CTX + PAD)
    cc = jnp.concatenate([c_ctx[None], c, jnp.zeros((6, D_MODEL), F32)], axis=0)
    mods = _mod_call(cc, w_mod, b_mod)
    xall = jnp.concatenate([x[0], ctx[0], jnp.zeros((PAD, D_MODEL), F32)], axis=0)

    for l in range(depth):
        last = l == depth - 1
        lam_init = 0.8 - 0.6 * math.exp(-0.3 * l)
        wl = _layer_weights(l, w_in, gqa_qk_gain, mla_cq_gain, mla_ckv_gain, mla_w_uq, mla_w_ukv,
                            mla_qk_gain, diff_qk_gain, w_branch, w_out, w_ffn_in, w_ffn_out)
        modsel = jnp.pad(mods[l, :2].reshape(2, 6, D_MODEL), ((0, 0), (0, 2), (0, 0)))
        g_attn = norm_gain[l, 0][None]
        g_ffn = norm_gain[l, 1][None]

        bounds = _score_bounds(gqa_qk_gain[l], mla_qk_gain[l], diff_qk_gain[l])
        safe = bounds <= SHIFT_MAX
        shift_tab = jnp.pad(jnp.broadcast_to(-jnp.minimum(bounds, SHIFT_MAX)[:, None], (3, LANES)),
                            ((0, 5), (0, 0)))
        p = _prep_call(xall, modsel, g_attn, wl, shift_tab, tabs, seq)
        dec = jnp.pad(ret_decay[l].astype(F32), ((0, 6), (0, LANES - N_HEADS_RET)))
        of, orv = _ret_call(p["qb"], p["kb"], p["vb"], dec, seq)
        oa = _attn_call(p["qa"], p["ka"], p["va"], safe[0], n_heads=8, n_kv=2, nm=1, dv=64, n_lat=seq,
                        tq=2 * ATT_TQ, name="attn_gqa")
        oc = _attn_call(p["qc"], p["kc"], p["vc"], safe[1], n_heads=8, n_kv=8, nm=1, dv=64, n_lat=seq,
                        tq=2 * ATT_TQ, name="attn_mla")
        lam_tab = jnp.pad(diff_lambda[l].astype(F32), ((0, 4), (0, LANES - 64)))
        od = _attn_call(p["qd"], p["kd"], p["vd"], safe[2], n_heads=4, n_kv=4, nm=2, dv=128, n_lat=seq,
                        tq=ATT_TQ, name="attn_diff", extra=(lam_tab, diff_subln_gain[l][None]),
                        diff_scale=1.0 - lam_init)
        n_rows = seq if last else n_all
        x1 = _merge_call(xall, modsel, g_attn, wl, oa, of, orv, p["gb"], oc, od, n_rows=n_rows, n_lat=seq)
        xall = _ffn_call(x1, modsel, g_ffn, wl, seq)

    return xall[None]
```
